```python
import jax, jax.numpy as jnp
from jax import lax
import numpy as np

D_MODEL = 1024
BATCH = 2
SEQ = 16384
DEPTH = 4

GRID_W = 64
CTX_LEN = 256
D_BRANCH = D_MODEL // 2
NA_HEAD_DIM = 64
NA_HEADS = D_BRANCH // NA_HEAD_DIM
WIN_ROWS = 8
WIN_COLS = 16
Q_COLS = 16
BAND_COLS = min(Q_COLS + WIN_COLS, GRID_W)
SHORT_CONV_W = 3
CONF_CONV_W = 31
N_BRANCH = 3
EPS = 1e-6
NEG_INF = -1e30

A_Q = 0 * D_BRANCH
A_K = 1 * D_BRANCH
A_V = 2 * D_BRANCH
A_Z = 3 * D_BRANCH
B_B = 4 * D_BRANCH
B_C = 5 * D_BRANCH
B_V = 6 * D_BRANCH
B_Z = 7 * D_BRANCH
C_GLU = 8 * D_BRANCH
C_Z = 10 * D_BRANCH
GATES = 11 * D_BRANCH
D_IN = GATES + N_BRANCH * D_MODEL

kernel_name = 'hybrid_na_shortconv_conformer_dit'


def rms_norm(x, g):
    x32 = x.astype(jnp.float32)
    y = x32 * lax.rsqrt(jnp.mean(x32 * x32, axis=-1, keepdims=True) + EPS)
    return (y * g.astype(jnp.float32)).astype(x.dtype)


def layer_norm(x, g, b):
    x32 = x.astype(jnp.float32)
    mu = jnp.mean(x32, axis=-1, keepdims=True)
    var = jnp.mean(jnp.square(x32 - mu), axis=-1, keepdims=True)
    y = (x32 - mu) * lax.rsqrt(var + EPS) * g.astype(jnp.float32) + b.astype(jnp.float32)
    return y.astype(x.dtype)


def dw_conv(x, w):
    k = w.shape[0]
    return lax.conv_general_dilated(
        x, w[:, None, :].astype(x.dtype), window_strides=(1,),
        padding=[((k - 1) // 2, k // 2)],
        dimension_numbers=('NWC', 'WIO', 'NWC'),
        feature_group_count=x.shape[-1])


def modulate(x, cond, norm_g, w_ada, b_ada):
    shift, scale, gate = jnp.split(jax.nn.silu(cond) @ w_ada + b_ada, 3, axis=-1)
    if cond.ndim == 2:
        shift, scale, gate = shift[:, None], scale[:, None], gate[:, None]
    return rms_norm(x, norm_g) * (1 + scale) + shift, gate


def heads(t):
    return t.reshape(t.shape[:-1] + (NA_HEADS, NA_HEAD_DIM))


def na_query(z, qn_g):
    return rms_norm(heads(z[..., A_Q:A_K]), qn_g) * NA_HEAD_DIM ** -0.5


def na_kv(zkv, kn_g):
    return rms_norm(heads(zkv[..., :D_BRANCH]), kn_g), heads(zkv[..., D_BRANCH:])


def neighbourhood_attention(q, k, v, kc, vc, rpb):
    bsz, seq, n_h, hd = q.shape
    rows = seq // GRID_W
    win_r = min(WIN_ROWS, rows)
    n_cb = GRID_W // Q_COLS
    qg = q.reshape(bsz, rows, n_cb, Q_COLS, n_h, hd)
    kg = k.reshape(bsz, rows, GRID_W, n_h, hd)
    vg = v.reshape(bsz, rows, GRID_W, n_h, hd)
    qcol = jnp.arange(GRID_W).reshape(n_cb, Q_COLS)
    col_start = jnp.clip(qcol - WIN_COLS // 2, 0, GRID_W - WIN_COLS)
    band_start = jnp.clip(jnp.arange(n_cb) * Q_COLS - WIN_COLS // 2, 0, GRID_W - BAND_COLS)
    band_cols = band_start[:, None] + jnp.arange(BAND_COLS)
    rel = band_cols[:, None, :] - col_start[:, :, None]
    col_valid = (rel >= 0) & (rel < WIN_COLS)
    dcol = jnp.clip(band_cols[:, None, :] - qcol[:, :, None], -(WIN_COLS - 1), WIN_COLS - 1) + WIN_COLS - 1
    n_loc = win_r * BAND_COLS

    def row_block(r):
        rs = jnp.clip(r - win_r // 2, 0, rows - win_r)
        q_r = lax.dynamic_index_in_dim(qg, r, axis=1, keepdims=False)
        k_r = lax.dynamic_slice_in_dim(kg, rs, win_r, axis=1)
        v_r = lax.dynamic_slice_in_dim(vg, rs, win_r, axis=1)
        kb = k_r[:, :, band_cols]
        vb = v_r[:, :, band_cols]
        s_loc = jnp.einsum('bcqhd,brcmhd->bhcqrm', q_r, kb).astype(jnp.float32)
        drow = rs + jnp.arange(win_r) - r + WIN_ROWS - 1
        bias = jnp.transpose(rpb[:, drow][:, :, dcol], (0, 2, 3, 1, 4))
        s_loc = jnp.where(col_valid[:, :, None, :], s_loc + bias.astype(jnp.float32), NEG_INF)
        s_ctx = jnp.einsum('bcqhd,blhd->bhcql', q_r, kc).astype(jnp.float32)
        s = jnp.concatenate([s_loc.reshape(bsz, n_h, n_cb, Q_COLS, n_loc), s_ctx], axis=-1)
        p = jax.nn.softmax(s, axis=-1).astype(v.dtype)
        p_loc = p[..., :n_loc].reshape(bsz, n_h, n_cb, Q_COLS, win_r, BAND_COLS)
        p_ctx = p[..., n_loc:]
        return (jnp.einsum('bhcqrm,brcmhd->bcqhd', p_loc, vb)
                + jnp.einsum('bhcql,blhd->bcqhd', p_ctx, vc))

    out = lax.map(row_block, jnp.arange(rows))
    return jnp.moveaxis(out, 0, 1).reshape(bsz, seq, n_h * hd)


def context_attention(qc, kc, vc):
    s = jnp.einsum('blhd,bmhd->bhlm', qc, kc).astype(jnp.float32)
    p = jax.nn.softmax(s, axis=-1).astype(vc.dtype)
    o = jnp.einsum('bhlm,bmhd->blhd', p, vc)
    return o.reshape(o.shape[:2] + (NA_HEADS * NA_HEAD_DIM,))


def short_conv_branch(z, w_conv):
    bg, cg, val, gz = z[..., B_B:B_C], z[..., B_C:B_V], z[..., B_V:B_Z], z[..., B_Z:C_GLU]
    return bg * dw_conv(cg * val, w_conv) * jax.nn.silu(gz)


def conformer_branch(z, w_conv, b_conv, ln_g, ln_b):
    a, g = z[..., C_GLU:C_GLU + D_BRANCH], z[..., C_GLU + D_BRANCH:C_Z]
    u = dw_conv(a * jax.nn.sigmoid(g), w_conv) + b_conv
    u = jax.nn.silu(layer_norm(u, ln_g, ln_b))
    return u * jax.nn.silu(z[..., C_Z:GATES])


def merge_branches(z, ya, yb, yc, w_out_a, w_out_b, w_out_c, w_o):
    ga, gb, gc = jnp.split(jax.nn.sigmoid(z[..., GATES:]), N_BRANCH, axis=-1)
    m = ga * (ya @ w_out_a) + gb * (yb @ w_out_b) + gc * (yc @ w_out_c)
    return m @ w_o


def hybrid_layer(x, xc, c, c_ctx, lp, update_ctx):
    h, gate = modulate(x, c, lp['norm_g'], lp['w_ada'], lp['b_ada'])
    hc, gate_c = modulate(xc, c_ctx, lp['norm_g'], lp['w_ada'], lp['b_ada'])
    z = h @ lp['w_in']
    if update_ctx:
        zc = hc @ lp['w_in']
        kc, vc = na_kv(zc[..., A_K:A_Z], lp['k_norm_g'])
    else:
        kc, vc = na_kv(hc @ lp['w_in'][:, A_K:A_Z], lp['k_norm_g'])

    q = na_query(z, lp['q_norm_g'])
    k, v = na_kv(z[..., A_K:A_Z], lp['k_norm_g'])
    ya = neighbourhood_attention(q, k, v, kc, vc, lp['rpb']) * jax.nn.silu(z[..., A_Z:B_B])
    yb = short_conv_branch(z, lp['conv_short_w'])
    yc = conformer_branch(z, lp['conv_conf_w'], lp['conv_conf_b'], lp['ln_conf_g'], lp['ln_conf_b'])
    x = x + gate * merge_branches(z, ya, yb, yc, lp['w_out_a'], lp['w_out_b'], lp['w_out_c'], lp['w_o'])

    if update_ctx:
        qc = na_query(zc, lp['q_norm_g'])
        yac = context_attention(qc, kc, vc) * jax.nn.silu(zc[..., A_Z:B_B])
        ybc = short_conv_branch(zc, lp['conv_short_w'])
        ycc = conformer_branch(zc, lp['conv_conf_w'], lp['conv_conf_b'], lp['ln_conf_g'], lp['ln_conf_b'])
        xc = xc + gate_c * merge_branches(zc, yac, ybc, ycc, lp['w_out_a'], lp['w_out_b'], lp['w_out_c'], lp['w_o'])
    return x, xc


def setup_inputs(seed: int = 0) -> dict:
    key = jax.random.key(seed)
    ks = jax.random.split(key, 20)

    def nrm(k, shape, scale):
        return jax.random.normal(k, shape, jnp.float32) * scale

    return {
        'x': nrm(ks[0], (BATCH, SEQ, D_MODEL), 1.0),
        'c': nrm(ks[1], (BATCH, D_MODEL), 1.0),
        'ctx': nrm(ks[2], (BATCH, CTX_LEN, D_MODEL), 1.0),
        'c_ctx': nrm(ks[3], (D_MODEL,), 1.0),
        'norm_g': 1.0 + nrm(ks[4], (DEPTH, D_MODEL), 0.05),
        'w_ada': nrm(ks[5], (DEPTH, D_MODEL, 3 * D_MODEL), D_MODEL ** -0.5),
        'b_ada': nrm(ks[6], (DEPTH, 3 * D_MODEL), 0.02),
        'w_in': nrm(ks[7], (DEPTH, D_MODEL, D_IN), D_MODEL ** -0.5),
        'q_norm_g': 1.0 + nrm(ks[8], (DEPTH, NA_HEAD_DIM), 0.05),
        'k_norm_g': 1.0 + nrm(ks[9], (DEPTH, NA_HEAD_DIM), 0.05),
        'rpb': nrm(ks[10], (DEPTH, NA_HEADS, 2 * WIN_ROWS - 1, 2 * WIN_COLS - 1), 0.1),
        'conv_short_w': nrm(ks[11], (DEPTH, SHORT_CONV_W, D_BRANCH), SHORT_CONV_W ** -0.5),
        'conv_conf_w': nrm(ks[12], (DEPTH, CONF_CONV_W, D_BRANCH), CONF_CONV_W ** -0.5),
        'conv_conf_b': nrm(ks[13], (DEPTH, D_BRANCH), 0.02),
        'ln_conf_g': 1.0 + nrm(ks[14], (DEPTH, D_BRANCH), 0.05),
        'ln_conf_b': nrm(ks[15], (DEPTH, D_BRANCH), 0.02),
        'w_out_a': nrm(ks[16], (DEPTH, D_BRANCH, D_MODEL), D_BRANCH ** -0.5),
        'w_out_b': nrm(ks[17], (DEPTH, D_BRANCH, D_MODEL), D_BRANCH ** -0.5),
        'w_out_c': nrm(ks[18], (DEPTH, D_BRANCH, D_MODEL), D_BRANCH ** -0.5),
        'w_o': nrm(ks[19], (DEPTH, D_MODEL, D_MODEL), D_MODEL ** -0.5),
    }


def reference(x, c, ctx, c_ctx, norm_g, w_ada, b_ada, w_in, q_norm_g, k_norm_g, rpb,
              conv_short_w, conv_conf_w, conv_conf_b, ln_conf_g, ln_conf_b,
              w_out_a, w_out_b, w_out_c, w_o):
    xc = ctx
    for l in range(DEPTH):
        lp = {
            'norm_g': norm_g[l], 'w_ada': w_ada[l], 'b_ada': b_ada[l], 'w_in': w_in[l],
            'q_norm_g': q_norm_g[l], 'k_norm_g': k_norm_g[l], 'rpb': rpb[l],
            'conv_short_w': conv_short_w[l], 'conv_conf_w': conv_conf_w[l],
            'conv_conf_b': conv_conf_b[l], 'ln_conf_g': ln_conf_g[l], 'ln_conf_b': ln_conf_b[l],
            'w_out_a': w_out_a[l], 'w_out_b': w_out_b[l], 'w_out_c': w_out_c[l], 'w_o': w_o[l],
        }
        x, xc = hybrid_layer(x, xc, c, c_ctx, lp, update_ctx=(l < DEPTH - 1))
    return x
```

```python
import functools

import numpy as np
import jax
import jax.numpy as jnp
from jax import lax
from jax.experimental import pallas as pl
from jax.experimental.pallas import tpu as pltpu

F32 = jnp.float32
BF16 = jnp.bfloat16

D_MODEL = 1024
D_BRANCH = 512
HEAD_DIM = 64
N_HEADS = 8
GRID_W = 64
WIN_ROWS = 8
WIN_COLS = 16
SHORT_W = 3
CONF_W = 31
N_BRANCH = 3
N_PROJ_GROUPS = 11 + 2 * N_BRANCH
EPS = 1e-6
NEG_INF = -1e30

HEADS_PER_GROUP = 4
GROUP_W = HEADS_PER_GROUP * HEAD_DIM
N_HGROUPS = N_HEADS // HEADS_PER_GROUP
N_BIAS_IDX = 2 * WIN_ROWS - 2
CONV_HALO = 16

TM_PROJ = 512
ROWS_MIX = 8
TM_MIX = ROWS_MIX * GRID_W
KV_HALO = 4 * GRID_W
CONV_CHUNK = 128

VMEM_LIMIT = 56 * 1024 * 1024


def _silu(x):
    return x * jax.nn.sigmoid(x)


def _modulate(x, g, scale, shift):
    y = x * lax.rsqrt(jnp.mean(x * x, axis=-1, keepdims=True) + EPS)
    return ((y * g) * (1.0 + scale) + shift).astype(BF16)


def _head_rms(z, pool, g):
    ms = jnp.dot((z * z).astype(BF16), pool, preferred_element_type=F32)
    return z * lax.rsqrt(ms + EPS) * g


def _proj_col(h, w_ref, i):
    return jnp.dot(h, w_ref[:, i * D_BRANCH:(i + 1) * D_BRANCH], preferred_element_type=F32)


def _head_masks():
    lane = lax.broadcasted_iota(jnp.int32, (1, GROUP_W), 1) // HEAD_DIM
    return [lane == h for h in range(HEADS_PER_GROUP)]


def _stack_heads(qg, masks):
    return jnp.concatenate([jnp.where(m, qg, jnp.zeros_like(qg)) for m in masks], axis=0)


def _unstack_heads(o, n, masks):
    out = jnp.where(masks[0], o[0:n], 0.0)
    for h in range(1, HEADS_PER_GROUP):
        out = out + jnp.where(masks[h], o[h * n:(h + 1) * n], 0.0)
    return out


def _dot_t(a, b):
    return lax.dot_general(a, b, (((1,), (1,)), ((), ())), preferred_element_type=F32)


def _dwconv(win_ref, w_ref, width, r0, n):
    half = (width - 1) // 2
    acc = None
    for j in range(width):
        off = CONV_HALO + r0 + j - half
        term = win_ref[off:off + n, :] * w_ref[j:j + 1, :]
        acc = term if acc is None else acc + term
    return acc


def _layer_norm(u, g, b):
    mu = jnp.mean(u, axis=-1, keepdims=True)
    d = u - mu
    var = jnp.mean(d * d, axis=-1, keepdims=True)
    return d * lax.rsqrt(var + EPS) * g + b


def _conv_branches(cwin, swin, sb_ref, sc_ref, wsc_ref, wcc_ref, bcc_ref, lng_ref, lnb_ref,
                   yb_ref, yc_ref, n_tokens):
    for r0 in range(0, n_tokens, CONV_CHUNK):
        n = min(CONV_CHUNK, n_tokens - r0)
        yb = _dwconv(swin, wsc_ref, SHORT_W, r0, n) * sb_ref[r0:r0 + n, :].astype(F32)
        yb_ref[r0:r0 + n, :] = yb.astype(BF16)
        u = _dwconv(cwin, wcc_ref, CONF_W, r0, n) + bcc_ref[...]
        u = _silu(_layer_norm(u, lng_ref[...], lnb_ref[...]))
        yc_ref[r0:r0 + n, :] = (u * sc_ref[r0:r0 + n, :].astype(F32)).astype(BF16)


def _merge(x, gate, ya, yb, yc, gates_ref, woa_ref, wob_ref, woc_ref, wo_ref):
    d = D_MODEL
    m = gates_ref[:, 0:d].astype(F32) * jnp.dot(ya, woa_ref[...], preferred_element_type=F32)
    m = m + gates_ref[:, d:2 * d].astype(F32) * jnp.dot(yb, wob_ref[...], preferred_element_type=F32)
    m = m + gates_ref[:, 2 * d:3 * d].astype(F32) * jnp.dot(yc, woc_ref[...], preferred_element_type=F32)
    return x + gate * jnp.dot(m.astype(BF16), wo_ref[...], preferred_element_type=F32)


def _ada_kernel(cond_ref, w_ref, b_ref, o_ref):
    o_ref[...] = jnp.dot(_silu(cond_ref[...]), w_ref[...], preferred_element_type=F32) + b_ref[...]


def _ada_call(cond, w_ada, b_ada):
    depth = w_ada.shape[0]
    n_rows = cond.shape[0]
    return pl.pallas_call(
        _ada_kernel,
        grid=(depth, 3),
        in_specs=[
            pl.BlockSpec((n_rows, D_MODEL), lambda l, j: (0, 0)),
            pl.BlockSpec((None, D_MODEL, D_MODEL), lambda l, j: (l, 0, j)),
            pl.BlockSpec((None, 1, D_MODEL), lambda l, j: (l, 0, j)),
        ],
        out_specs=pl.BlockSpec((None, n_rows, D_MODEL), lambda l, j: (l, 0, j)),
        out_shape=jax.ShapeDtypeStruct((depth, n_rows, 3 * D_MODEL), F32),
        name="ada_mod",
    )(cond, w_ada, b_ada.reshape(depth, 1, 3 * D_MODEL))


def _inproj_kernel(x_ref, g_ref, scale_ref, shift_ref, w_ref, pool_ref, qg_ref, kg_ref,
                   q_o, k_o, v_o, sa_o, sb_o, cv_o, glu_o, sc_o, gates_o):
    h = _modulate(x_ref[...], g_ref[...], scale_ref[...], shift_ref[...])
    pool = pool_ref[...]
    q_o[...] = (_head_rms(_proj_col(h, w_ref, 0), pool, qg_ref[...]) * HEAD_DIM ** -0.5).astype(BF16)
    k_o[...] = _head_rms(_proj_col(h, w_ref, 1), pool, kg_ref[...]).astype(BF16)
    v_o[...] = _proj_col(h, w_ref, 2).astype(BF16)
    sa_o[...] = _silu(_proj_col(h, w_ref, 3)).astype(BF16)
    sb_o[...] = (_proj_col(h, w_ref, 4) * _silu(_proj_col(h, w_ref, 7))).astype(BF16)
    cv_o[...] = (_proj_col(h, w_ref, 5) * _proj_col(h, w_ref, 6)).astype(BF16)
    glu_o[...] = (_proj_col(h, w_ref, 8) * jax.nn.sigmoid(_proj_col(h, w_ref, 9))).astype(BF16)
    sc_o[...] = _silu(_proj_col(h, w_ref, 10)).astype(BF16)
    for i in range(2 * N_BRANCH):
        gates_o[:, i * D_BRANCH:(i + 1) * D_BRANCH] = jax.nn.sigmoid(_proj_col(h, w_ref, 11 + i)).astype(BF16)


def _const_spec(shape):
    nd = len(shape)
    return pl.BlockSpec(shape, lambda *_: (0,) * nd, pipeline_mode=pl.Buffered(1))


def _inproj_call(x, norm_g, scale, shift, w_in, pool, qg, kg):
    bsz, seq, d = x.shape
    d_in = w_in.shape[1]
    tok = lambda w: pl.BlockSpec((None, TM_PROJ, w), lambda b, i: (b, i, 0))
    vec = pl.BlockSpec((None, 1, d), lambda b, i: (b, 0, 0))
    widths = [D_BRANCH] * 8 + [N_BRANCH * D_MODEL]
    return pl.pallas_call(
        _inproj_kernel,
        grid=(bsz, seq // TM_PROJ),
        in_specs=[tok(d), _const_spec((1, d)), vec, vec, _const_spec((d, d_in)),
                  _const_spec((D_BRANCH, D_BRANCH)), _const_spec((1, D_BRANCH)), _const_spec((1, D_BRANCH))],
        out_specs=[tok(w) for w in widths],
        out_shape=[jax.ShapeDtypeStruct((bsz, seq, w), BF16) for w in widths],
        compiler_params=pltpu.CompilerParams(
            dimension_semantics=("parallel", "parallel"), vmem_limit_bytes=VMEM_LIMIT),
        name="in_proj",
    )(x, norm_g, scale, shift, w_in, pool, qg, kg)


def _mixer_kernel(x_ref, q_ref, kp_ref, k_ref, kn_ref, vp_ref, v_ref, vn_ref, kc_ref, vc_ref,
                  bias_ref, sa_ref, sb_ref, cvp_ref, cv_ref, cvn_ref, glup_ref, glu_ref, glun_ref,
                  sc_ref, gates_ref, gate_ref, wsc_ref, wcc_ref, bcc_ref, lng_ref, lnb_ref,
                  woa_ref, wob_ref, woc_ref, wo_ref, out_ref,
                  kwin, vwin, cwin, swin, ya_ref, yb_ref, yc_ref):
    i = pl.program_id(1)
    n_tiles = pl.num_programs(1)
    n_rows = n_tiles * ROWS_MIX

    kwin[0:KV_HALO, :] = kp_ref[...]
    kwin[KV_HALO:KV_HALO + TM_MIX, :] = k_ref[...]
    kwin[KV_HALO + TM_MIX:, :] = kn_ref[...]
    vwin[0:KV_HALO, :] = vp_ref[...]
    vwin[KV_HALO:KV_HALO + TM_MIX, :] = v_ref[...]
    vwin[KV_HALO + TM_MIX:, :] = vn_ref[...]

    has_prev = (i > 0).astype(F32)
    has_next = (i < n_tiles - 1).astype(F32)
    for win, prev, cur, nxt in ((cwin, glup_ref, glu_ref, glun_ref), (swin, cvp_ref, cv_ref, cvn_ref)):
        win[0:CONV_HALO, :] = prev[...].astype(F32) * has_prev
        win[CONV_HALO:CONV_HALO + TM_MIX, :] = cur[...].astype(F32)
        win[CONV_HALO + TM_MIX:, :] = nxt[...].astype(F32) * has_next

    masks = _head_masks()
    tile_row0 = i * ROWS_MIX

    def attend_row(rr, carry):
        row = tile_row0 + rr
        row_start = jnp.clip(row - WIN_ROWS // 2, 0, n_rows - WIN_ROWS)
        delta = row - row_start
        k0 = pl.multiple_of((row_start - tile_row0 + 4) * GRID_W, GRID_W)
        q0 = pl.multiple_of(rr * GRID_W, GRID_W)
        q_row = q_ref[pl.ds(q0, GRID_W), :]
        outs = []
        for g in range(N_HGROUPS):
            lanes = slice(g * GROUP_W, (g + 1) * GROUP_W)
            qm = _stack_heads(q_row[:, lanes], masks)
            s_loc = _dot_t(qm, kwin[pl.ds(k0, WIN_ROWS * GRID_W), lanes])
            s_ctx = _dot_t(qm, kc_ref[:, lanes])
            bias = jnp.concatenate(
                [bias_ref[g, 2 * j + WIN_ROWS - 1 - delta] for j in range(WIN_ROWS // 2)], axis=-1)
            s_loc = s_loc + bias
            m = jnp.maximum(jnp.max(s_loc, axis=-1, keepdims=True), jnp.max(s_ctx, axis=-1, keepdims=True))
            p_loc = jnp.exp(s_loc - m)
            p_ctx = jnp.exp(s_ctx - m)
            denom = jnp.sum(p_loc, axis=-1, keepdims=True) + jnp.sum(p_ctx, axis=-1, keepdims=True)
            o = jnp.dot(p_loc.astype(BF16), vwin[pl.ds(k0, WIN_ROWS * GRID_W), lanes],
                        preferred_element_type=F32)
            o = o + jnp.dot(p_ctx.astype(BF16), vc_ref[:, lanes], preferred_element_type=F32)
            outs.append(_unstack_heads(o * (1.0 / denom), GRID_W, masks))
        att = jnp.concatenate(outs, axis=-1)
        ya_ref[pl.ds(q0, GRID_W), :] = (att * sa_ref[pl.ds(q0, GRID_W), :].astype(F32)).astype(BF16)
        return carry

    lax.fori_loop(0, ROWS_MIX, attend_row, 0)

    _conv_branches(cwin, swin, sb_ref, sc_ref, wsc_ref, wcc_ref, bcc_ref, lng_ref, lnb_ref,
                   yb_ref, yc_ref, TM_MIX)

    out_ref[...] = _merge(x_ref[...], gate_ref[...], ya_ref[...], yb_ref[...], yc_ref[...],
                          gates_ref, woa_ref, wob_ref, woc_ref, wo_ref)


def _mixer_call(x, q, k, v, kc, vc, bias, sa, sb, cv, glu, sc, gates, gate,
                wsc, wcc, bcc, lng, lnb, woa, wob, woc, wo):
    bsz, seq, d = x.shape
    n_kv_blocks = seq // KV_HALO
    kv_per_tile = TM_MIX // KV_HALO
    n_cv_blocks = seq // CONV_HALO
    cv_per_tile = TM_MIX // CONV_HALO
    ctx_len = kc.shape[1]

    tok = lambda w: pl.BlockSpec((None, TM_MIX, w), lambda b, i: (b, i, 0))
    kv_prev = pl.BlockSpec((None, KV_HALO, D_BRANCH),
                           lambda b, i: (b, jnp.maximum(i * kv_per_tile - 1, 0), 0))
    kv_next = pl.BlockSpec((None, KV_HALO, D_BRANCH),
                           lambda b, i: (b, jnp.minimum((i + 1) * kv_per_tile, n_kv_blocks - 1), 0))
    cv_prev = pl.BlockSpec((None, CONV_HALO, D_BRANCH),
                           lambda b, i: (b, jnp.maximum(i * cv_per_tile - 1, 0), 0))
    cv_next = pl.BlockSpec((None, CONV_HALO, D_BRANCH),
                           lambda b, i: (b, jnp.minimum((i + 1) * cv_per_tile, n_cv_blocks - 1), 0))
    ctx_spec = pl.BlockSpec((None, ctx_len, D_BRANCH), lambda b, i: (b, 0, 0))
    vec = pl.BlockSpec((None, 1, d), lambda b, i: (b, 0, 0))
    tb = tok(D_BRANCH)

    in_specs = [
        tok(d), tb, kv_prev, tb, kv_next, kv_prev, tb, kv_next, ctx_spec, ctx_spec,
        _const_spec(bias.shape), tb, tb, cv_prev, tb, cv_next, cv_prev, tb, cv_next,
        tb, tok(N_BRANCH * d), vec,
        _const_spec(wsc.shape), _const_spec(wcc.shape), _const_spec(bcc.shape),
        _const_spec(lng.shape), _const_spec(lnb.shape),
        _const_spec(woa.shape), _const_spec(wob.shape), _const_spec(woc.shape), _const_spec(wo.shape),
    ]
    return pl.pallas_call(
        _mixer_kernel,
        grid=(bsz, seq // TM_MIX),
        in_specs=in_specs,
        out_specs=tok(d),
        out_shape=jax.ShapeDtypeStruct((bsz, seq, d), F32),
        scratch_shapes=[
            pltpu.VMEM((TM_MIX + 2 * KV_HALO, D_BRANCH), BF16),
            pltpu.VMEM((TM_MIX + 2 * KV_HALO, D_BRANCH), BF16),
            pltpu.VMEM((TM_MIX + 2 * CONV_HALO, D_BRANCH), F32),
            pltpu.VMEM((TM_MIX + 2 * CONV_HALO, D_BRANCH), F32),
            pltpu.VMEM((TM_MIX, D_BRANCH), BF16),
            pltpu.VMEM((TM_MIX, D_BRANCH), BF16),
            pltpu.VMEM((TM_MIX, D_BRANCH), BF16),
        ],
        compiler_params=pltpu.CompilerParams(
            dimension_semantics=("parallel", "parallel"), vmem_limit_bytes=VMEM_LIMIT),
        name="mixer",
    )(x, q, k, k, k, v, v, v, kc, vc, bias, sa, sb, cv, cv, cv, glu, glu, glu, sc, gates, gate,
      wsc, wcc, bcc, lng, lnb, woa, wob, woc, wo)


def _ctx_kernel(update, xc_ref, g_ref, scale_ref, shift_ref, gate_ref, w_ref, pool_ref, qg_ref, kg_ref,
                wsc_ref, wcc_ref, bcc_ref, lng_ref, lnb_ref, woa_ref, wob_ref, woc_ref, wo_ref,
                *refs):
    if update:
        kc_o, vc_o, xc_o, cwin, swin, sb_s, sc_s, ya_s, yb_s, yc_s, gates_s = refs
    else:
        kc_o, vc_o = refs
    n = xc_ref.shape[0]
    xc = xc_ref[...]
    h = _modulate(xc, g_ref[...], scale_ref[...], shift_ref[...])
    pool = pool_ref[...]
    kc = _head_rms(_proj_col(h, w_ref, 1), pool, kg_ref[...]).astype(BF16)
    vc = _proj_col(h, w_ref, 2).astype(BF16)
    kc_o[...] = kc
    vc_o[...] = vc
    if not update:
        return

    q = (_head_rms(_proj_col(h, w_ref, 0), pool, qg_ref[...]) * HEAD_DIM ** -0.5).astype(BF16)
    sa = _silu(_proj_col(h, w_ref, 3))
    masks = _head_masks()
    outs = []
    for g in range(N_HGROUPS):
        lanes = slice(g * GROUP_W, (g + 1) * GROUP_W)
        qm = _stack_heads(q[:, lanes], masks)
        s = _dot_t(qm, kc[:, lanes])
        p = jnp.exp(s - jnp.max(s, axis=-1, keepdims=True))
        o = jnp.dot(p.astype(BF16), vc[:, lanes], preferred_element_type=F32)
        outs.append(_unstack_heads(o * (1.0 / jnp.sum(p, axis=-1, keepdims=True)), n, masks))
    ya_s[...] = (jnp.concatenate(outs, axis=-1) * sa).astype(BF16)

    sb_s[...] = (_proj_col(h, w_ref, 4) * _silu(_proj_col(h, w_ref, 7))).astype(BF16)
    sc_s[...] = _silu(_proj_col(h, w_ref, 10)).astype(BF16)
    zeros = jnp.zeros((CONV_HALO, D_BRANCH), F32)
    for win, val in ((swin, _proj_col(h, w_ref, 5) * _proj_col(h, w_ref, 6)),
                     (cwin, _proj_col(h, w_ref, 8) * jax.nn.sigmoid(_proj_col(h, w_ref, 9)))):
        win[0:CONV_HALO, :] = zeros
        win[CONV_HALO:CONV_HALO + n, :] = val
        win[CONV_HALO + n:, :] = zeros
    _conv_branches(cwin, swin, sb_s, sc_s, wsc_ref, wcc_ref, bcc_ref, lng_ref, lnb_ref, yb_s, yc_s, n)
    for i in range(2 * N_BRANCH):
        gates_s[:, i * D_BRANCH:(i + 1) * D_BRANCH] = jax.nn.sigmoid(_proj_col(h, w_ref, 11 + i)).astype(BF16)
    xc_o[...] = _merge(xc, gate_ref[...], ya_s[...], yb_s[...], yc_s[...],
                       gates_s, woa_ref, wob_ref, woc_ref, wo_ref)


def _ctx_call(update, xc, norm_g, scale, shift, gate, w_in, pool, qg, kg,
              wsc, wcc, bcc, lng, lnb, woa, wob, woc, wo):
    bsz, n, d = xc.shape
    tok = lambda w: pl.BlockSpec((None, n, w), lambda b: (b, 0, 0))
    consts = (norm_g, scale, shift, gate, w_in, pool, qg, kg, wsc, wcc, bcc, lng, lnb, woa, wob, woc, wo)
    out_specs = [tok(D_BRANCH), tok(D_BRANCH)]
    out_shape = [jax.ShapeDtypeStruct((bsz, n, D_BRANCH), BF16)] * 2
    scratch = []
    if update:
        out_specs.append(tok(d))
        out_shape.append(jax.ShapeDtypeStruct((bsz, n, d), F32))
        scratch = [pltpu.VMEM((n + 2 * CONV_HALO, D_BRANCH), F32)] * 2 \
            + [pltpu.VMEM((n, D_BRANCH), BF16)] * 5 + [pltpu.VMEM((n, N_BRANCH * d), BF16)]
    return pl.pallas_call(
        functools.partial(_ctx_kernel, update),
        grid=(bsz,),
        in_specs=[tok(d)] + [_const_spec(a.shape) for a in consts],
        out_specs=out_specs,
        out_shape=out_shape,
        scratch_shapes=scratch,
        compiler_params=pltpu.CompilerParams(
            dimension_semantics=("parallel",), vmem_limit_bytes=VMEM_LIMIT),
        name="ctx_layer_update" if update else "ctx_layer_kv",
    )(xc, *consts)


def _bias_index_tables():
    qc = np.arange(GRID_W)[:, None]
    kc = np.arange(GRID_W)[None, :]
    col_start = np.clip(qc - WIN_COLS // 2, 0, GRID_W - WIN_COLS)
    valid = (kc - col_start >= 0) & (kc - col_start < WIN_COLS)
    dcol = np.clip(kc - qc, -(WIN_COLS - 1), WIN_COLS - 1) + WIN_COLS - 1
    return valid, dcol


def _bias_table(rpb_l):
    valid, dcol = _bias_index_tables()
    a = jnp.where(valid[None, None], rpb_l[:, :, dcol], NEG_INF)
    pair = jnp.concatenate([a[:, :N_BIAS_IDX], a[:, 1:N_BIAS_IDX + 1]], axis=-1)
    pair = pair.reshape(N_HGROUPS, HEADS_PER_GROUP, N_BIAS_IDX, GRID_W, 2 * GRID_W)
    return jnp.transpose(pair, (0, 2, 1, 3, 4)).reshape(
        N_HGROUPS, N_BIAS_IDX, HEADS_PER_GROUP * GRID_W, 2 * GRID_W).astype(F32)


def _pool_matrix():
    head = np.arange(D_BRANCH) // HEAD_DIM
    return jnp.asarray((head[:, None] == head[None, :]).astype(np.float32) / HEAD_DIM, dtype=BF16)


def kernel(x, c, ctx, c_ctx, norm_g, w_ada, b_ada, w_in, q_norm_g, k_norm_g, rpb, conv_short_w,
           conv_conf_w, conv_conf_b, ln_conf_g, ln_conf_b, w_out_a, w_out_b, w_out_c, w_o):
    depth = w_in.shape[0]
    bsz, seq, d = x.shape
    assert (d, seq % TM_MIX, seq % TM_PROJ) == (D_MODEL, 0, 0)

    n_cond = 8
    cond = jnp.zeros((n_cond, d), F32).at[:bsz].set(c).at[bsz].set(c_ctx)
    mods = _ada_call(cond, w_ada, b_ada)
    pool = _pool_matrix()
    rep = N_HEADS

    xc = ctx
    for l in range(depth):
        shift, scale, gate = (mods[l, :, j * d:(j + 1) * d] for j in range(3))
        lat = lambda t: t[:bsz].reshape(bsz, 1, d)
        cx = lambda t: t[bsz:bsz + 1]
        g_l = norm_g[l].reshape(1, d)
        w_l = w_in[l].astype(BF16)
        qg = jnp.tile(q_norm_g[l], rep).reshape(1, D_BRANCH)
        kg = jnp.tile(k_norm_g[l], rep).reshape(1, D_BRANCH)
        tail = (conv_short_w[l], conv_conf_w[l], conv_conf_b[l].reshape(1, D_BRANCH),
                ln_conf_g[l].reshape(1, D_BRANCH), ln_conf_b[l].reshape(1, D_BRANCH),
                w_out_a[l].astype(BF16), w_out_b[l].astype(BF16), w_out_c[l].astype(BF16),
                w_o[l].astype(BF16))

        update = l < depth - 1
        ctx_out = _ctx_call(update, xc, g_l, cx(scale), cx(shift), cx(gate), w_l, pool, qg, kg, *tail)
        kc, vc = ctx_out[0], ctx_out[1]

        q, k, v, sa, sb, cv, glu, sc, gates = _inproj_call(
            x, g_l, lat(scale), lat(shift), w_l, pool, qg, kg)
        x = _mixer_call(x, q, k, v, kc, vc, _bias_table(rpb[l]), sa, sb, cv, glu, sc, gates,
                        lat(gate), *tail)
        if update:
            xc = ctx_out[2]
    return x
```

```python
import functools

import numpy as np
import jax
import jax.numpy as jnp
from jax import lax
from jax.experimental import pallas as pl
from jax.experimental.pallas import tpu as pltpu

F32 = jnp.float32
BF16 = jnp.bfloat16

D_MODEL = 1024
D_BRANCH = 512
HEAD_DIM = 64
N_HEADS = 8
GRID_W = 64
WIN_ROWS = 8
WIN_COLS = 16
SHORT_W = 3
CONF_W = 31
N_BRANCH = 3
N_PROJ_GROUPS = 11 + 2 * N_BRANCH
EPS = 1e-6
NEG_INF = -1e30

HEADS_PER_GROUP = 4
GROUP_W = HEADS_PER_GROUP * HEAD_DIM
N_HGROUPS = N_HEADS // HEADS_PER_GROUP
N_BIAS_IDX = 2 * WIN_ROWS - 2
CONV_HALO = 16

TM_PROJ = 512
ROWS_MIX = 8
TM_MIX = ROWS_MIX * GRID_W
KV_HALO = 4 * GRID_W
CONV_CHUNK = 64
LANES = 128
SUBLANES = 8

VMEM_LIMIT = 56 * 1024 * 1024


def _silu(x):
    return x * jax.nn.sigmoid(x)


def _modulate(x, g, scale, shift):
    y = x * lax.rsqrt(jnp.mean(x * x, axis=-1, keepdims=True) + EPS)
    return ((y * g) * (1.0 + scale) + shift).astype(BF16)


def _head_rms(z, pool, g):
    ms = jnp.dot((z * z).astype(BF16), pool, preferred_element_type=F32)
    return z * lax.rsqrt(ms + EPS) * g


def _proj_col(h, w_ref, i):
    return jnp.dot(h, w_ref[:, i * D_BRANCH:(i + 1) * D_BRANCH], preferred_element_type=F32)


def _head_masks():
    lane = lax.broadcasted_iota(jnp.int32, (1, GROUP_W), 1) // HEAD_DIM
    return [lane == h for h in range(HEADS_PER_GROUP)]


def _stack_heads(qg, masks):
    return jnp.concatenate([jnp.where(m, qg, jnp.zeros_like(qg)) for m in masks], axis=0)


def _unstack_heads(o, n, masks):
    out = jnp.where(masks[0], o[0:n], 0.0)
    for h in range(1, HEADS_PER_GROUP):
        out = out + jnp.where(masks[h], o[h * n:(h + 1) * n], 0.0)
    return out


def _dot_t(a, b):
    return lax.dot_general(a, b, (((1,), (1,)), ((), ())), preferred_element_type=F32)


def _dwconv(win_ref, w_ref, width, r0, n, c0):
    half = (width - 1) // 2
    lanes = slice(c0, c0 + LANES)
    offsets = [CONV_HALO + j - half for j in range(width)]
    span = n + SUBLANES * (max(offsets) // SUBLANES + 1)
    base = win_ref[r0:r0 + span, lanes]
    shifted = {0: base}
    acc = None
    for j, o in enumerate(offsets):
        s, q = o % SUBLANES, o // SUBLANES
        if s not in shifted:
            shifted[s] = pltpu.roll(base, span - s, 0)
        term = shifted[s][SUBLANES * q:SUBLANES * q + n] * w_ref[j:j + 1, lanes]
        acc = term if acc is None else acc + term
    return acc


def _layer_norm(u, g, b):
    mu = jnp.mean(u, axis=-1, keepdims=True)
    d = u - mu
    var = jnp.mean(d * d, axis=-1, keepdims=True)
    return d * lax.rsqrt(var + EPS) * g + b


def _conv_branches(cwin, swin, sb_ref, sc_ref, wsc_ref, wcc_ref, bcc_ref, lng_ref, lnb_ref,
                   yb_ref, yc_ref, n_tokens):
    cols = range(0, D_BRANCH, LANES)
    for r0 in range(0, n_tokens, CONV_CHUNK):
        n = min(CONV_CHUNK, n_tokens - r0)
        rows = slice(r0, r0 + n)
        for c0 in cols:
            yb = _dwconv(swin, wsc_ref, SHORT_W, r0, n, c0) * sb_ref[rows, c0:c0 + LANES].astype(F32)
            yb_ref[rows, c0:c0 + LANES] = yb.astype(BF16)
        u = jnp.concatenate([_dwconv(cwin, wcc_ref, CONF_W, r0, n, c0) for c0 in cols], axis=-1)
        u = _silu(_layer_norm(u + bcc_ref[...], lng_ref[...], lnb_ref[...]))
        yc_ref[rows, :] = (u * sc_ref[rows, :].astype(F32)).astype(BF16)


def _merge(x, gate, ya, yb, yc, gates_ref, woa_ref, wob_ref, woc_ref, wo_ref):
    d = D_MODEL
    m = gates_ref[:, 0:d].astype(F32) * jnp.dot(ya, woa_ref[...], preferred_element_type=F32)
    m = m + gates_ref[:, d:2 * d].astype(F32) * jnp.dot(yb, wob_ref[...], preferred_element_type=F32)
    m = m + gates_ref[:, 2 * d:3 * d].astype(F32) * jnp.dot(yc, woc_ref[...], preferred_element_type=F32)
    return x + gate * jnp.dot(m.astype(BF16), wo_ref[...], preferred_element_type=F32)


def _ada_kernel(cond_ref, w_ref, b_ref, o_ref):
    o_ref[...] = jnp.dot(_silu(cond_ref[...]), w_ref[...], preferred_element_type=F32) + b_ref[...]


def _ada_call(cond, w_ada, b_ada):
    depth = w_ada.shape[0]
    n_rows = cond.shape[0]
    return pl.pallas_call(
        _ada_kernel,
        grid=(depth, 3),
        in_specs=[
            pl.BlockSpec((n_rows, D_MODEL), lambda l, j: (0, 0)),
            pl.BlockSpec((None, D_MODEL, D_MODEL), lambda l, j: (l, 0, j)),
            pl.BlockSpec((None, 1, D_MODEL), lambda l, j: (l, 0, j)),
        ],
        out_specs=pl.BlockSpec((None, n_rows, D_MODEL), lambda l, j: (l, 0, j)),
        out_shape=jax.ShapeDtypeStruct((depth, n_rows, 3 * D_MODEL), F32),
        name="ada_mod",
    )(cond, w_ada, b_ada.reshape(depth, 1, 3 * D_MODEL))


def _inproj_kernel(x_ref, g_ref, scale_ref, shift_ref, w_ref, pool_ref, qg_ref, kg_ref,
                   q_o, k_o, v_o, sa_o, sb_o, cv_o, glu_o, sc_o, gates_o):
    h = _modulate(x_ref[...], g_ref[...], scale_ref[...], shift_ref[...])
    pool = pool_ref[...]
    q_o[...] = (_head_rms(_proj_col(h, w_ref, 0), pool, qg_ref[...]) * HEAD_DIM ** -0.5).astype(BF16)
    k_o[...] = _head_rms(_proj_col(h, w_ref, 1), pool, kg_ref[...]).astype(BF16)
    v_o[...] = _proj_col(h, w_ref, 2).astype(BF16)
    sa_o[...] = _silu(_proj_col(h, w_ref, 3)).astype(BF16)
    sb_o[...] = (_proj_col(h, w_ref, 4) * _silu(_proj_col(h, w_ref, 7))).astype(BF16)
    cv_o[...] = (_proj_col(h, w_ref, 5) * _proj_col(h, w_ref, 6)).astype(BF16)
    glu_o[...] = (_proj_col(h, w_ref, 8) * jax.nn.sigmoid(_proj_col(h, w_ref, 9))).astype(BF16)
    sc_o[...] = _silu(_proj_col(h, w_ref, 10)).astype(BF16)
    for i in range(2 * N_BRANCH):
        gates_o[:, i * D_BRANCH:(i + 1) * D_BRANCH] = jax.nn.sigmoid(_proj_col(h, w_ref, 11 + i)).astype(BF16)


def _const_spec(shape):
    nd = len(shape)
    return pl.BlockSpec(shape, lambda *_: (0,) * nd, pipeline_mode=pl.Buffered(1))


def _inproj_call(x, norm_g, scale, shift, w_in, pool, qg, kg):
    bsz, seq, d = x.shape
    d_in = w_in.shape[1]
    tok = lambda w: pl.BlockSpec((None, TM_PROJ, w), lambda b, i: (b, i, 0))
    vec = pl.BlockSpec((None, 1, d), lambda b, i: (b, 0, 0))
    widths = [D_BRANCH] * 8 + [N_BRANCH * D_MODEL]
    return pl.pallas_call(
        _inproj_kernel,
        grid=(bsz, seq // TM_PROJ),
        in_specs=[tok(d), _const_spec((1, d)), vec, vec, _const_spec((d, d_in)),
                  _const_spec((D_BRANCH, D_BRANCH)), _const_spec((1, D_BRANCH)), _const_spec((1, D_BRANCH))],
        out_specs=[tok(w) for w in widths],
        out_shape=[jax.ShapeDtypeStruct((bsz, seq, w), BF16) for w in widths],
        compiler_params=pltpu.CompilerParams(
            dimension_semantics=("parallel", "parallel"), vmem_limit_bytes=VMEM_LIMIT),
        name="in_proj",
    )(x, norm_g, scale, shift, w_in, pool, qg, kg)


def _mixer_kernel(x_ref, q_ref, kp_ref, k_ref, kn_ref, vp_ref, v_ref, vn_ref, kc_ref, vc_ref,
                  bias_ref, sa_ref, sb_ref, cvp_ref, cv_ref, cvn_ref, glup_ref, glu_ref, glun_ref,
                  sc_ref, gates_ref, gate_ref, wsc_ref, wcc_ref, bcc_ref, lng_ref, lnb_ref,
                  woa_ref, wob_ref, woc_ref, wo_ref, out_ref,
                  kwin, vwin, cwin, swin, ya_ref, yb_ref, yc_ref):
    i = pl.program_id(1)
    n_tiles = pl.num_programs(1)
    n_rows = n_tiles * ROWS_MIX

    kwin[0:KV_HALO, :] = kp_ref[...]
    kwin[KV_HALO:KV_HALO + TM_MIX, :] = k_ref[...]
    kwin[KV_HALO + TM_MIX:, :] = kn_ref[...]
    vwin[0:KV_HALO, :] = vp_ref[...]
    vwin[KV_HALO:KV_HALO + TM_MIX, :] = v_ref[...]
    vwin[KV_HALO + TM_MIX:, :] = vn_ref[...]

    has_prev = (i > 0).astype(F32)
    has_next = (i < n_tiles - 1).astype(F32)
    for win, prev, cur, nxt in ((cwin, glup_ref, glu_ref, glun_ref), (swin, cvp_ref, cv_ref, cvn_ref)):
        win[0:CONV_HALO, :] = prev[...].astype(F32) * has_prev
        win[CONV_HALO:CONV_HALO + TM_MIX, :] = cur[...].astype(F32)
        win[CONV_HALO + TM_MIX:, :] = nxt[...].astype(F32) * has_next

    masks = _head_masks()
    tile_row0 = i * ROWS_MIX

    def attend_row(rr, carry):
        row = tile_row0 + rr
        row_start = jnp.clip(row - WIN_ROWS // 2, 0, n_rows - WIN_ROWS)
        delta = row - row_start
        k0 = pl.multiple_of((row_start - tile_row0 + 4) * GRID_W, GRID_W)
        q0 = pl.multiple_of(rr * GRID_W, GRID_W)
        q_row = q_ref[pl.ds(q0, GRID_W), :]
        outs = []
        for g in range(N_HGROUPS):
            lanes = slice(g * GROUP_W, (g + 1) * GROUP_W)
            qm = _stack_heads(q_row[:, lanes], masks)
            s_loc = _dot_t(qm, kwin[pl.ds(k0, WIN_ROWS * GRID_W), lanes])
            s_ctx = _dot_t(qm, kc_ref[:, lanes])
            bias = jnp.concatenate(
                [bias_ref[g, 2 * j + WIN_ROWS - 1 - delta] for j in range(WIN_ROWS // 2)], axis=-1)
            s_loc = s_loc + bias
            m = jnp.maximum(jnp.max(s_loc, axis=-1, keepdims=True), jnp.max(s_ctx, axis=-1, keepdims=True))
            p_loc = jnp.exp(s_loc - m)
            p_ctx = jnp.exp(s_ctx - m)
            denom = jnp.sum(p_loc, axis=-1, keepdims=True) + jnp.sum(p_ctx, axis=-1, keepdims=True)
            o = jnp.dot(p_loc.astype(BF16), vwin[pl.ds(k0, WIN_ROWS * GRID_W), lanes],
                        preferred_element_type=F32)
            o = o + jnp.dot(p_ctx.astype(BF16), vc_ref[:, lanes], preferred_element_type=F32)
            outs.append(_unstack_heads(o * (1.0 / denom), GRID_W, masks))
        att = jnp.concatenate(outs, axis=-1)
        ya_ref[pl.ds(q0, GRID_W), :] = (att * sa_ref[pl.ds(q0, GRID_W), :].astype(F32)).astype(BF16)
        return carry

    lax.fori_loop(0, ROWS_MIX, attend_row, 0, unroll=True)

    _conv_branches(cwin, swin, sb_ref, sc_ref, wsc_ref, wcc_ref, bcc_ref, lng_ref, lnb_ref,
                   yb_ref, yc_ref, TM_MIX)

    out_ref[...] = _merge(x_ref[...], gate_ref[...], ya_ref[...], yb_ref[...], yc_ref[...],
                          gates_ref, woa_ref, wob_ref, woc_ref, wo_ref)


def _mixer_call(x, q, k, v, kc, vc, bias, sa, sb, cv, glu, sc, gates, gate,
                wsc, wcc, bcc, lng, lnb, woa, wob, woc, wo):
    bsz, seq, d = x.shape
    n_kv_blocks = seq // KV_HALO
    kv_per_tile = TM_MIX // KV_HALO
    n_cv_blocks = seq // CONV_HALO
    cv_per_tile = TM_MIX // CONV_HALO
    ctx_len = kc.shape[1]

    tok = lambda w: pl.BlockSpec((None, TM_MIX, w), lambda b, i: (b, i, 0))
    kv_prev = pl.BlockSpec((None, KV_HALO, D_BRANCH),
                           lambda b, i: (b, jnp.maximum(i * kv_per_tile - 1, 0), 0))
    kv_next = pl.BlockSpec((None, KV_HALO, D_BRANCH),
                           lambda b, i: (b, jnp.minimum((i + 1) * kv_per_tile, n_kv_blocks - 1), 0))
    cv_prev = pl.BlockSpec((None, CONV_HALO, D_BRANCH),
                           lambda b, i: (b, jnp.maximum(i * cv_per_tile - 1, 0), 0))
    cv_next = pl.BlockSpec((None, CONV_HALO, D_BRANCH),
                           lambda b, i: (b, jnp.minimum((i + 1) * cv_per_tile, n_cv_blocks - 1), 0))
    ctx_spec = pl.BlockSpec((None, ctx_len, D_BRANCH), lambda b, i: (b, 0, 0))
    vec = pl.BlockSpec((None, 1, d), lambda b, i: (b, 0, 0))
    tb = tok(D_BRANCH)

    in_specs = [
        tok(d), tb, kv_prev, tb, kv_next, kv_prev, tb, kv_next, ctx_spec, ctx_spec,
        _const_spec(bias.shape), tb, tb, cv_prev, tb, cv_next, cv_prev, tb, cv_next,
        tb, tok(N_BRANCH * d), vec,
        _const_spec(wsc.shape), _const_spec(wcc.shape), _const_spec(bcc.shape),
        _const_spec(lng.shape), _const_spec(lnb.shape),
        _const_spec(woa.shape), _const_spec(wob.shape), _const_spec(woc.shape), _const_spec(wo.shape),
    ]
    return pl.pallas_call(
        _mixer_kernel,
        grid=(bsz, seq // TM_MIX),
        in_specs=in_specs,
        out_specs=tok(d),
        out_shape=jax.ShapeDtypeStruct((bsz, seq, d), F32),
        scratch_shapes=[
            pltpu.VMEM((TM_MIX + 2 * KV_HALO, D_BRANCH), BF16),
            pltpu.VMEM((TM_MIX + 2 * KV_HALO, D_BRANCH), BF16),
            pltpu.VMEM((TM_MIX + 2 * CONV_HALO, D_BRANCH), F32),
            pltpu.VMEM((TM_MIX + 2 * CONV_HALO, D_BRANCH), F32),
            pltpu.VMEM((TM_MIX, D_BRANCH), BF16),
            pltpu.VMEM((TM_MIX, D_BRANCH), BF16),
            pltpu.VMEM((TM_MIX, D_BRANCH), BF16),
        ],
        compiler_params=pltpu.CompilerParams(
            dimension_semantics=("parallel", "parallel"), vmem_limit_bytes=VMEM_LIMIT),
        name="mixer",
    )(x, q, k, k, k, v, v, v, kc, vc, bias, sa, sb, cv, cv, cv, glu, glu, glu, sc, gates, gate,
      wsc, wcc, bcc, lng, lnb, woa, wob, woc, wo)


def _ctx_kernel(update, xc_ref, g_ref, scale_ref, shift_ref, gate_ref, w_ref, pool_ref, qg_ref, kg_ref,
                wsc_ref, wcc_ref, bcc_ref, lng_ref, lnb_ref, woa_ref, wob_ref, woc_ref, wo_ref,
                *refs):
    if update:
        kc_o, vc_o, xc_o, cwin, swin, sb_s, sc_s, ya_s, yb_s, yc_s, gates_s = refs
    else:
        kc_o, vc_o = refs
    n = xc_ref.shape[0]
    xc = xc_ref[...]
    h = _modulate(xc, g_ref[...], scale_ref[...], shift_ref[...])
    pool = pool_ref[...]
    kc = _head_rms(_proj_col(h, w_ref, 1), pool, kg_ref[...]).astype(BF16)
    vc = _proj_col(h, w_ref, 2).astype(BF16)
    kc_o[...] = kc
    vc_o[...] = vc
    if not update:
        return

    q = (_head_rms(_proj_col(h, w_ref, 0), pool, qg_ref[...]) * HEAD_DIM ** -0.5).astype(BF16)
    sa = _silu(_proj_col(h, w_ref, 3))
    masks = _head_masks()
    outs = []
    for g in range(N_HGROUPS):
        lanes = slice(g * GROUP_W, (g + 1) * GROUP_W)
        qm = _stack_heads(q[:, lanes], masks)
        s = _dot_t(qm, kc[:, lanes])
        p = jnp.exp(s - jnp.max(s, axis=-1, keepdims=True))
        o = jnp.dot(p.astype(BF16), vc[:, lanes], preferred_element_type=F32)
        outs.append(_unstack_heads(o * (1.0 / jnp.sum(p, axis=-1, keepdims=True)), n, masks))
    ya_s[...] = (jnp.concatenate(outs, axis=-1) * sa).astype(BF16)

    sb_s[...] = (_proj_col(h, w_ref, 4) * _silu(_proj_col(h, w_ref, 7))).astype(BF16)
    sc_s[...] = _silu(_proj_col(h, w_ref, 10)).astype(BF16)
    zeros = jnp.zeros((CONV_HALO, D_BRANCH), F32)
    for win, val in ((swin, _proj_col(h, w_ref, 5) * _proj_col(h, w_ref, 6)),
                     (cwin, _proj_col(h, w_ref, 8) * jax.nn.sigmoid(_proj_col(h, w_ref, 9)))):
        win[0:CONV_HALO, :] = zeros
        win[CONV_HALO:CONV_HALO + n, :] = val
        win[CONV_HALO + n:, :] = zeros
    _conv_branches(cwin, swin, sb_s, sc_s, wsc_ref, wcc_ref, bcc_ref, lng_ref, lnb_ref, yb_s, yc_s, n)
    for i in range(2 * N_BRANCH):
        gates_s[:, i * D_BRANCH:(i + 1) * D_BRANCH] = jax.nn.sigmoid(_proj_col(h, w_ref, 11 + i)).astype(BF16)
    xc_o[...] = _merge(xc, gate_ref[...], ya_s[...], yb_s[...], yc_s[...],
                       gates_s, woa_ref, wob_ref, woc_ref, wo_ref)


def _ctx_call(update, xc, norm_g, scale, shift, gate, w_in, pool, qg, kg,
              wsc, wcc, bcc, lng, lnb, woa, wob, woc, wo):
    bsz, n, d = xc.shape
    tok = lambda w: pl.BlockSpec((None, n, w), lambda b: (b, 0, 0))
    consts = (norm_g, scale, shift, gate, w_in, pool, qg, kg, wsc, wcc, bcc, lng, lnb, woa, wob, woc, wo)
    out_specs = [tok(D_BRANCH), tok(D_BRANCH)]
    out_shape = [jax.ShapeDtypeStruct((bsz, n, D_BRANCH), BF16)] * 2
    scratch = []
    if update:
        out_specs.append(tok(d))
        out_shape.append(jax.ShapeDtypeStruct((bsz, n, d), F32))
        scratch = [pltpu.VMEM((n + 2 * CONV_HALO, D_BRANCH), F32)] * 2 \
            + [pltpu.VMEM((n, D_BRANCH), BF16)] * 5 + [pltpu.VMEM((n, N_BRANCH * d), BF16)]
    return pl.pallas_call(
        functools.partial(_ctx_kernel, update),
        grid=(bsz,),
        in_specs=[tok(d)] + [_const_spec(a.shape) for a in consts],
        out_specs=out_specs,
        out_shape=out_shape,
        scratch_shapes=scratch,
        compiler_params=pltpu.CompilerParams(
            dimension_semantics=("parallel",), vmem_limit_bytes=VMEM_LIMIT),
        name="ctx_layer_update" if update else "ctx_layer_kv",
    )(xc, *consts)


def _bias_index_tables():
    qc = np.arange(GRID_W)[:, None]
    kc = np.arange(GRID_W)[None, :]
    col_start = np.clip(qc - WIN_COLS // 2, 0, GRID_W - WIN_COLS)
    valid = (kc - col_start >= 0) & (kc - col_start < WIN_COLS)
    dcol = np.clip(kc - qc, -(WIN_COLS - 1), WIN_COLS - 1) + WIN_COLS - 1
    return valid, dcol


def _bias_table(rpb_l):
    valid, dcol = _bias_index_tables()
    a = jnp.where(valid[None, None], rpb_l[:, :, dcol], NEG_INF)
    pair = jnp.concatenate([a[:, :N_BIAS_IDX], a[:, 1:N_BIAS_IDX + 1]], axis=-1)
    pair = pair.reshape(N_HGROUPS, HEADS_PER_GROUP, N_BIAS_IDX, GRID_W, 2 * GRID_W)
    return jnp.transpose(pair, (0, 2, 1, 3, 4)).reshape(
        N_HGROUPS, N_BIAS_IDX, HEADS_PER_GROUP * GRID_W, 2 * GRID_W).astype(F32)


def _pool_matrix():
    head = np.arange(D_BRANCH) // HEAD_DIM
    return jnp.asarray((head[:, None] == head[None, :]).astype(np.float32) / HEAD_DIM, dtype=BF16)


def kernel(x, c, ctx, c_ctx, norm_g, w_ada, b_ada, w_in, q_norm_g, k_norm_g, rpb, conv_short_w,
           conv_conf_w, conv_conf_b, ln_conf_g, ln_conf_b, w_out_a, w_out_b, w_out_c, w_o):
    depth = w_in.shape[0]
    bsz, seq, d = x.shape
    assert (d, seq % TM_MIX, seq % TM_PROJ) == (D_MODEL, 0, 0)

    n_cond = 8
    cond = jnp.zeros((n_cond, d), F32).at[:bsz].set(c).at[bsz].set(c_ctx)
    mods = _ada_call(cond, w_ada, b_ada)
    pool = _pool_matrix()
    rep = N_HEADS

    xc = ctx
    for l in range(depth):
        shift, scale, gate = (mods[l, :, j * d:(j + 1) * d] for j in range(3))
        lat = lambda t: t[:bsz].reshape(bsz, 1, d)
        cx = lambda t: t[bsz:bsz + 1]
        g_l = norm_g[l].reshape(1, d)
        w_l = w_in[l].astype(BF16)
        qg = jnp.tile(q_norm_g[l], rep).reshape(1, D_BRANCH)
        kg = jnp.tile(k_norm_g[l], rep).reshape(1, D_BRANCH)
        tail = (conv_short_w[l], conv_conf_w[l], conv_conf_b[l].reshape(1, D_BRANCH),
                ln_conf_g[l].reshape(1, D_BRANCH), ln_conf_b[l].reshape(1, D_BRANCH),
                w_out_a[l].astype(BF16), w_out_b[l].astype(BF16), w_out_c[l].astype(BF16),
                w_o[l].astype(BF16))

        update = l < depth - 1
        ctx_out = _ctx_call(update, xc, g_l, cx(scale), cx(shift), cx(gate), w_l, pool, qg, kg, *tail)
        kc, vc = ctx_out[0], ctx_out[1]

        q, k, v, sa, sb, cv, glu, sc, gates = _inproj_call(
            x, g_l, lat(scale), lat(shift), w_l, pool, qg, kg)
        x = _mixer_call(x, q, k, v, kc, vc, _bias_table(rpb[l]), sa, sb, cv, glu, sc, gates,
                        lat(gate), *tail)
        if update:
            xc = ctx_out[2]
    return x
```

```python
import functools

import numpy as np
import jax
import jax.numpy as jnp
from jax import lax
from jax.experimental import pallas as pl
from jax.experimental.pallas import tpu as pltpu

F32 = jnp.float32
BF16 = jnp.bfloat16

D_MODEL = 1024
D_BRANCH = 512
HEAD_DIM = 64
N_HEADS = 8
GRID_W = 64
WIN_ROWS = 8
WIN_COLS = 16
SHORT_W = 3
CONF_W = 31
N_BRANCH = 3
N_PROJ_GROUPS = 11 + 2 * N_BRANCH
EPS = 1e-6
NEG_INF = -1e30

HEADS_PER_GROUP = 4
GROUP_W = HEADS_PER_GROUP * HEAD_DIM
N_HGROUPS = N_HEADS // HEADS_PER_GROUP
N_BIAS_IDX = 2 * WIN_ROWS - 2
CONV_HALO = 16

TM_PROJ = 512
ROWS_MIX = 8
TM_MIX = ROWS_MIX * GRID_W
KV_HALO = 4 * GRID_W
CONV_CHUNK = 64
LANES = 128
SUBLANES = 8

VMEM_LIMIT = 56 * 1024 * 1024


def _silu(x):
    return x * jax.nn.sigmoid(x)


def _modulate(x, g, scale, shift):
    y = x * lax.rsqrt(jnp.mean(x * x, axis=-1, keepdims=True) + EPS)
    return ((y * g) * (1.0 + scale) + shift).astype(BF16)


def _head_rms(z, pool, g):
    ms = jnp.dot((z * z).astype(BF16), pool, preferred_element_type=F32)
    return z * lax.rsqrt(ms + EPS) * g


def _proj_col(h, w_ref, i):
    return jnp.dot(h, w_ref[:, i * D_BRANCH:(i + 1) * D_BRANCH], preferred_element_type=F32)


def _head_masks():
    lane = lax.broadcasted_iota(jnp.int32, (1, GROUP_W), 1) // HEAD_DIM
    return [lane == h for h in range(HEADS_PER_GROUP)]


def _stack_heads(qg, masks):
    return jnp.concatenate([jnp.where(m, qg, jnp.zeros_like(qg)) for m in masks], axis=0)


def _unstack_heads(o, n, masks):
    out = jnp.where(masks[0], o[0:n], 0.0)
    for h in range(1, HEADS_PER_GROUP):
        out = out + jnp.where(masks[h], o[h * n:(h + 1) * n], 0.0)
    return out


def _dot_t(a, b):
    return lax.dot_general(a, b, (((1,), (1,)), ((), ())), preferred_element_type=F32)


def _ordering_zero(v):
    u = pltpu.bitcast(v, jnp.uint32)
    return pltpu.bitcast(lax.shift_right_logical(lax.shift_right_logical(u, jnp.uint32(16)), jnp.uint32(16)), F32)


def _dwconv(win_ref, w_ref, width, r0, n, c0, after=None):
    half = (width - 1) // 2
    lanes = slice(c0, c0 + LANES)
    offsets = [CONV_HALO + j - half for j in range(width)]
    span = n + SUBLANES * (max(offsets) // SUBLANES + 1)
    base = win_ref[r0:r0 + span, lanes]
    if after is not None:
        base = base + jnp.tile(after, (span // SUBLANES, 1))
    acc = None
    for s in range(SUBLANES):
        taps = [(j, o // SUBLANES) for j, o in enumerate(offsets) if o % SUBLANES == s]
        if not taps:
            continue
        shifted = pltpu.roll(base, span - s, 0) if s else base
        for j, q in taps:
            term = shifted[SUBLANES * q:SUBLANES * q + n] * w_ref[j:j + 1, lanes]
            acc = term if acc is None else acc + term
    return acc


def _layer_norm(u, g, b):
    mu = jnp.mean(u, axis=-1, keepdims=True)
    d = u - mu
    var = jnp.mean(d * d, axis=-1, keepdims=True)
    return d * lax.rsqrt(var + EPS) * g + b


def _short_conv_chunk(swin, sb_ref, wsc_ref, yb_ref, r0, n):
    rows = slice(r0, r0 + n)
    for c0 in range(0, D_BRANCH, LANES):
        yb = _dwconv(swin, wsc_ref, SHORT_W, r0, n, c0) * sb_ref[rows, c0:c0 + LANES].astype(F32)
        yb_ref[rows, c0:c0 + LANES] = yb.astype(BF16)


def _conf_conv_lanes(cwin, wcc_ref, r0, n, lane_blocks, after=None):
    return [_dwconv(cwin, wcc_ref, CONF_W, r0, n, b * LANES, after) for b in lane_blocks]


def _conf_finish(u_blocks, sc_ref, bcc_ref, lng_ref, lnb_ref, yc_ref, r0, n):
    rows = slice(r0, r0 + n)
    u = jnp.concatenate(u_blocks, axis=-1)
    u = _silu(_layer_norm(u + bcc_ref[...], lng_ref[...], lnb_ref[...]))
    yc = u * sc_ref[rows, :].astype(F32)
    yc_ref[rows, :] = yc.astype(BF16)
    return yc


def _conf_conv_chunk(cwin, sc_ref, wcc_ref, bcc_ref, lng_ref, lnb_ref, yc_ref, r0, n):
    u_blocks = _conf_conv_lanes(cwin, wcc_ref, r0, n, range(D_BRANCH // LANES))
    return _conf_finish(u_blocks, sc_ref, bcc_ref, lng_ref, lnb_ref, yc_ref, r0, n)


def _conv_branches(cwin, swin, sb_ref, sc_ref, wsc_ref, wcc_ref, bcc_ref, lng_ref, lnb_ref,
                   yb_ref, yc_ref, n_tokens):
    for r0 in range(0, n_tokens, CONV_CHUNK):
        n = min(CONV_CHUNK, n_tokens - r0)
        _short_conv_chunk(swin, sb_ref, wsc_ref, yb_ref, r0, n)
        _conf_conv_chunk(cwin, sc_ref, wcc_ref, bcc_ref, lng_ref, lnb_ref, yc_ref, r0, n)


def _merge(x, gate, ya, yb, yc, gates_ref, woa_ref, wob_ref, woc_ref, wo_ref):
    d = D_MODEL
    m = gates_ref[:, 0:d].astype(F32) * jnp.dot(ya, woa_ref[...], preferred_element_type=F32)
    m = m + gates_ref[:, d:2 * d].astype(F32) * jnp.dot(yb, wob_ref[...], preferred_element_type=F32)
    m = m + gates_ref[:, 2 * d:3 * d].astype(F32) * jnp.dot(yc, woc_ref[...], preferred_element_type=F32)
    return x + gate * jnp.dot(m.astype(BF16), wo_ref[...], preferred_element_type=F32)


def _ada_kernel(cond_ref, w_ref, b_ref, o_ref):
    o_ref[...] = jnp.dot(_silu(cond_ref[...]), w_ref[...], preferred_element_type=F32) + b_ref[...]


def _ada_call(cond, w_ada, b_ada):
    depth = w_ada.shape[0]
    n_rows = cond.shape[0]
    return pl.pallas_call(
        _ada_kernel,
        grid=(depth, 3),
        in_specs=[
            pl.BlockSpec((n_rows, D_MODEL), lambda l, j: (0, 0)),
            pl.BlockSpec((None, D_MODEL, D_MODEL), lambda l, j: (l, 0, j)),
            pl.BlockSpec((None, 1, D_MODEL), lambda l, j: (l, 0, j)),
        ],
        out_specs=pl.BlockSpec((None, n_rows, D_MODEL), lambda l, j: (l, 0, j)),
        out_shape=jax.ShapeDtypeStruct((depth, n_rows, 3 * D_MODEL), F32),
        name="ada_mod",
    )(cond, w_ada, b_ada.reshape(depth, 1, 3 * D_MODEL))


def _inproj_kernel(xp_ref, x_ref, xn_ref, g_ref, scale_ref, shift_ref, w_ref, pool_ref, qg_ref, kg_ref,
                   wsc_ref, wcc_ref, bcc_ref, lng_ref, lnb_ref,
                   q_o, k_o, v_o, sa_o, yb_o, yc_o, gates_o, h_s, cwin, swin, sb_s, sc_s):
    i = pl.program_id(1)
    mod = (g_ref[...], scale_ref[...], shift_ref[...])
    tile = slice(CONV_HALO, CONV_HALO + TM_PROJ)
    h_s[0:CONV_HALO, :] = _modulate(xp_ref[...], *mod)
    h_s[tile, :] = _modulate(x_ref[...], *mod)
    h_s[CONV_HALO + TM_PROJ:, :] = _modulate(xn_ref[...], *mod)
    col = lambda c: _proj_col(h_s[tile, :], w_ref, c)
    col_ext = lambda c: _proj_col(h_s[...], w_ref, c)

    has_prev = (i > 0).astype(F32)
    has_next = (i < pl.num_programs(1) - 1).astype(F32)

    def fill_window(win, val):
        win[0:CONV_HALO, :] = val[0:CONV_HALO] * has_prev
        win[tile, :] = val[tile]
        win[CONV_HALO + TM_PROJ:, :] = val[CONV_HALO + TM_PROJ:] * has_next

    started = lambda z: _ordering_zero(z[0:SUBLANES, 0:LANES])
    probe = (slice(CONV_HALO, 2 * CONV_HALO), slice(0, LANES))
    n_lane_blocks = D_BRANCH // LANES
    conv_parts, done = {}, {}

    def conf_half(hf, after):
        c, second = divmod(hf, 2)
        r0 = c * CONV_CHUNK
        blocks = range(n_lane_blocks // 2, n_lane_blocks) if second else range(n_lane_blocks // 2)
        u = _conf_conv_lanes(cwin, wcc_ref, r0, CONV_CHUNK, blocks, after)
        if second:
            yc = _conf_finish(conv_parts.pop(c) + u, sc_s, bcc_ref, lng_ref, lnb_ref, yc_o, r0, CONV_CHUNK)
            done[hf] = yc[0:CONV_HALO, 0:LANES]
        else:
            conv_parts[c] = u
            done[hf] = u[0][0:CONV_HALO]

    def wait_half(hf):
        h_s[probe] = h_s[probe] + _ordering_zero(done.pop(hf)).astype(BF16)

    def short_chunks(*cs):
        for c in cs:
            _short_conv_chunk(swin, sb_s, wsc_ref, yb_o, c * CONV_CHUNK, CONV_CHUNK)

    pool = pool_ref[...]
    pending = {}

    def group_cv_a():
        pending["z5"] = col_ext(5)
        return pending["z5"]

    def group_cv_b():
        z6 = col_ext(6)
        fill_window(swin, pending.pop("z5") * z6)
        return z6

    def group_sb_a():
        pending["z4"] = col(4)
        return pending["z4"]

    def group_sb_b():
        z7 = col(7)
        sb_s[...] = pending.pop("z4") * _silu(z7)
        return z7

    def group_q():
        z = col(0)
        q_o[...] = (_head_rms(z, pool, qg_ref[...]) * HEAD_DIM ** -0.5).astype(BF16)
        return z

    def group_k():
        z = col(1)
        k_o[...] = _head_rms(z, pool, kg_ref[...]).astype(BF16)
        return z

    def group_v():
        z = col(2)
        v_o[...] = z.astype(BF16)
        return z

    def group_sa():
        z = col(3)
        sa_o[...] = _silu(z).astype(BF16)
        return z

    def gate_group(j):
        z = col(11 + j)
        gates_o[:, j * D_BRANCH:(j + 1) * D_BRANCH] = jax.nn.sigmoid(z).astype(BF16)
        return z

    groups = [group_cv_a, group_cv_b, group_sb_a, group_sb_b, group_q, group_k, group_v, group_sa]
    groups += [functools.partial(gate_group, j) for j in range(2 * N_BRANCH)]
    waits = {k: [k - 1] for k in range(1, 11)}
    waits.update({11: [10, 11], 12: [12, 13], 13: [14]})
    anchored = {k: [k + 1] for k in range(0, 9)}
    anchored.update({9: [10, 11], 10: [12, 13], 11: [14], 12: [15]})
    shorts = {5: (0, 1), 7: (2, 3), 9: (4, 5), 11: (6, 7)}

    fill_window(cwin, col_ext(8) * jax.nn.sigmoid(col_ext(9)))
    z10 = col(10)
    sc_s[...] = _silu(z10)
    conf_half(0, started(z10))
    for k, group in enumerate(groups):
        for hf in waits.get(k, ()):
            wait_half(hf)
        z = group()
        for hf in anchored.get(k, ()):
            conf_half(hf, started(z))
        short_chunks(*shorts.get(k, ()))


def _const_spec(shape):
    nd = len(shape)
    return pl.BlockSpec(shape, lambda *_: (0,) * nd, pipeline_mode=pl.Buffered(1))


def _inproj_call(x, norm_g, scale, shift, w_in, pool, qg, kg, wsc, wcc, bcc, lng, lnb):
    bsz, seq, d = x.shape
    halo_per_tile = TM_PROJ // CONV_HALO
    n_halo_blocks = seq // CONV_HALO
    tok = lambda w: pl.BlockSpec((None, TM_PROJ, w), lambda b, i: (b, i, 0))
    x_prev = pl.BlockSpec((None, CONV_HALO, d), lambda b, i: (b, jnp.maximum(i * halo_per_tile - 1, 0), 0))
    x_next = pl.BlockSpec((None, CONV_HALO, d),
                          lambda b, i: (b, jnp.minimum((i + 1) * halo_per_tile, n_halo_blocks - 1), 0))
    vec = pl.BlockSpec((None, 1, d), lambda b, i: (b, 0, 0))
    consts = (w_in, pool, qg, kg, wsc, wcc, bcc, lng, lnb)
    widths = [D_BRANCH] * 6 + [N_BRANCH * D_MODEL]
    return pl.pallas_call(
        _inproj_kernel,
        grid=(bsz, seq // TM_PROJ),
        in_specs=[x_prev, tok(d), x_next, _const_spec((1, d)), vec, vec] + [_const_spec(a.shape) for a in consts],
        out_specs=[tok(w) for w in widths],
        out_shape=[jax.ShapeDtypeStruct((bsz, seq, w), BF16) for w in widths],
        scratch_shapes=[pltpu.VMEM((TM_PROJ + 2 * CONV_HALO, d), BF16)]
        + [pltpu.VMEM((TM_PROJ + 2 * CONV_HALO, D_BRANCH), F32)] * 2
        + [pltpu.VMEM((TM_PROJ, D_BRANCH), F32)] * 2,
        compiler_params=pltpu.CompilerParams(
            dimension_semantics=("parallel", "parallel"), vmem_limit_bytes=VMEM_LIMIT),
        name="in_proj",
    )(x, x, x, norm_g, scale, shift, *consts)


def _mixer_kernel(x_ref, q_ref, kp_ref, k_ref, kn_ref, vp_ref, v_ref, vn_ref, kc_ref, vc_ref,
                  bias_ref, sa_ref, yb_ref, yc_ref, gates_ref, gate_ref,
                  woa_ref, wob_ref, woc_ref, wo_ref, out_ref, kwin, vwin, ya_ref):
    i = pl.program_id(1)
    n_rows = pl.num_programs(1) * ROWS_MIX

    kwin[0:KV_HALO, :] = kp_ref[...]
    kwin[KV_HALO:KV_HALO + TM_MIX, :] = k_ref[...]
    kwin[KV_HALO + TM_MIX:, :] = kn_ref[...]
    vwin[0:KV_HALO, :] = vp_ref[...]
    vwin[KV_HALO:KV_HALO + TM_MIX, :] = v_ref[...]
    vwin[KV_HALO + TM_MIX:, :] = vn_ref[...]

    masks = _head_masks()
    tile_row0 = i * ROWS_MIX

    def attend_row(rr, carry):
        row = tile_row0 + rr
        row_start = jnp.clip(row - WIN_ROWS // 2, 0, n_rows - WIN_ROWS)
        delta = row - row_start
        k0 = pl.multiple_of((row_start - tile_row0 + 4) * GRID_W, GRID_W)
        q0 = pl.multiple_of(rr * GRID_W, GRID_W)
        q_row = q_ref[pl.ds(q0, GRID_W), :]
        outs = []
        for g in range(N_HGROUPS):
            lanes = slice(g * GROUP_W, (g + 1) * GROUP_W)
            qm = _stack_heads(q_row[:, lanes], masks)
            s_loc = _dot_t(qm, kwin[pl.ds(k0, WIN_ROWS * GRID_W), lanes])
            s_ctx = _dot_t(qm, kc_ref[:, lanes])
            bias = jnp.concatenate(
                [bias_ref[g, 2 * j + WIN_ROWS - 1 - delta] for j in range(WIN_ROWS // 2)], axis=-1)
            s_loc = s_loc + bias
            m = jnp.maximum(jnp.max(s_loc, axis=-1, keepdims=True), jnp.max(s_ctx, axis=-1, keepdims=True))
            p_loc = jnp.exp(s_loc - m)
            p_ctx = jnp.exp(s_ctx - m)
            denom = jnp.sum(p_loc, axis=-1, keepdims=True) + jnp.sum(p_ctx, axis=-1, keepdims=True)
            o = jnp.dot(p_loc.astype(BF16), vwin[pl.ds(k0, WIN_ROWS * GRID_W), lanes],
                        preferred_element_type=F32)
            o = o + jnp.dot(p_ctx.astype(BF16), vc_ref[:, lanes], preferred_element_type=F32)
            outs.append(_unstack_heads(o * (1.0 / denom), GRID_W, masks))
        att = jnp.concatenate(outs, axis=-1)
        ya_ref[pl.ds(q0, GRID_W), :] = (att * sa_ref[pl.ds(q0, GRID_W), :].astype(F32)).astype(BF16)
        return carry

    lax.fori_loop(0, ROWS_MIX, attend_row, 0, unroll=True)

    out_ref[...] = _merge(x_ref[...], gate_ref[...], ya_ref[...], yb_ref[...], yc_ref[...],
                          gates_ref, woa_ref, wob_ref, woc_ref, wo_ref)


def _mixer_call(x, q, k, v, kc, vc, bias, sa, yb, yc, gates, gate, woa, wob, woc, wo):
    bsz, seq, d = x.shape
    n_kv_blocks = seq // KV_HALO
    kv_per_tile = TM_MIX // KV_HALO
    ctx_len = kc.shape[1]

    tok = lambda w: pl.BlockSpec((None, TM_MIX, w), lambda b, i: (b, i, 0))
    kv_prev = pl.BlockSpec((None, KV_HALO, D_BRANCH),
                           lambda b, i: (b, jnp.maximum(i * kv_per_tile - 1, 0), 0))
    kv_next = pl.BlockSpec((None, KV_HALO, D_BRANCH),
                           lambda b, i: (b, jnp.minimum((i + 1) * kv_per_tile, n_kv_blocks - 1), 0))
    ctx_spec = pl.BlockSpec((None, ctx_len, D_BRANCH), lambda b, i: (b, 0, 0))
    vec = pl.BlockSpec((None, 1, d), lambda b, i: (b, 0, 0))
    tb = tok(D_BRANCH)
    consts = (woa, wob, woc, wo)

    in_specs = [tok(d), tb, kv_prev, tb, kv_next, kv_prev, tb, kv_next, ctx_spec, ctx_spec,
                _const_spec(bias.shape), tb, tb, tb, tok(N_BRANCH * d), vec] + [_const_spec(a.shape) for a in consts]
    return pl.pallas_call(
        _mixer_kernel,
        grid=(bsz, seq // TM_MIX),
        in_specs=in_specs,
        out_specs=tok(d),
        out_shape=jax.ShapeDtypeStruct((bsz, seq, d), F32),
        scratch_shapes=[
            pltpu.VMEM((TM_MIX + 2 * KV_HALO, D_BRANCH), BF16),
            pltpu.VMEM((TM_MIX + 2 * KV_HALO, D_BRANCH), BF16),
            pltpu.VMEM((TM_MIX, D_BRANCH), BF16),
        ],
        compiler_params=pltpu.CompilerParams(
            dimension_semantics=("parallel", "parallel"), vmem_limit_bytes=VMEM_LIMIT),
        name="mixer",
    )(x, q, k, k, k, v, v, v, kc, vc, bias, sa, yb, yc, gates, gate, *consts)


def _ctx_kernel(update, xc_ref, g_ref, scale_ref, shift_ref, gate_ref, w_ref, pool_ref, qg_ref, kg_ref,
                wsc_ref, wcc_ref, bcc_ref, lng_ref, lnb_ref, woa_ref, wob_ref, woc_ref, wo_ref,
                *refs):
    if update:
        kc_o, vc_o, xc_o, cwin, swin, sb_s, sc_s, ya_s, yb_s, yc_s, gates_s = refs
    else:
        kc_o, vc_o = refs
    n = xc_ref.shape[0]
    xc = xc_ref[...]
    h = _modulate(xc, g_ref[...], scale_ref[...], shift_ref[...])
    pool = pool_ref[...]
    kc = _head_rms(_proj_col(h, w_ref, 1), pool, kg_ref[...]).astype(BF16)
    vc = _proj_col(h, w_ref, 2).astype(BF16)
    kc_o[...] = kc
    vc_o[...] = vc
    if not update:
        return

    q = (_head_rms(_proj_col(h, w_ref, 0), pool, qg_ref[...]) * HEAD_DIM ** -0.5).astype(BF16)
    sa = _silu(_proj_col(h, w_ref, 3))
    masks = _head_masks()
    outs = []
    for g in range(N_HGROUPS):
        lanes = slice(g * GROUP_W, (g + 1) * GROUP_W)
        qm = _stack_heads(q[:, lanes], masks)
        s = _dot_t(qm, kc[:, lanes])
        p = jnp.exp(s - jnp.max(s, axis=-1, keepdims=True))
        o = jnp.dot(p.astype(BF16), vc[:, lanes], preferred_element_type=F32)
        outs.append(_unstack_heads(o * (1.0 / jnp.sum(p, axis=-1, keepdims=True)), n, masks))
    ya_s[...] = (jnp.concatenate(outs, axis=-1) * sa).astype(BF16)

    sb_s[...] = (_proj_col(h, w_ref, 4) * _silu(_proj_col(h, w_ref, 7))).astype(BF16)
    sc_s[...] = _silu(_proj_col(h, w_ref, 10)).astype(BF16)
    zeros = jnp.zeros((CONV_HALO, D_BRANCH), F32)
    for win, val in ((swin, _proj_col(h, w_ref, 5) * _proj_col(h, w_ref, 6)),
                     (cwin, _proj_col(h, w_ref, 8) * jax.nn.sigmoid(_proj_col(h, w_ref, 9)))):
        win[0:CONV_HALO, :] = zeros
        win[CONV_HALO:CONV_HALO + n, :] = val
        win[CONV_HALO + n:, :] = zeros
    _conv_branches(cwin, swin, sb_s, sc_s, wsc_ref, wcc_ref, bcc_ref, lng_ref, lnb_ref, yb_s, yc_s, n)
    for i in range(2 * N_BRANCH):
        gates_s[:, i * D_BRANCH:(i + 1) * D_BRANCH] = jax.nn.sigmoid(_proj_col(h, w_ref, 11 + i)).astype(BF16)
    xc_o[...] = _merge(xc, gate_ref[...], ya_s[...], yb_s[...], yc_s[...],
                       gates_s, woa_ref, wob_ref, woc_ref, wo_ref)


def _ctx_call(update, xc, norm_g, scale, shift, gate, w_in, pool, qg, kg,
              wsc, wcc, bcc, lng, lnb, woa, wob, woc, wo):
    bsz, n, d = xc.shape
    tok = lambda w: pl.BlockSpec((None, n, w), lambda b: (b, 0, 0))
    consts = (norm_g, scale, shift, gate, w_in, pool, qg, kg, wsc, wcc, bcc, lng, lnb, woa, wob, woc, wo)
    out_specs = [tok(D_BRANCH), tok(D_BRANCH)]
    out_shape = [jax.ShapeDtypeStruct((bsz, n, D_BRANCH), BF16)] * 2
    scratch = []
    if update:
        out_specs.append(tok(d))
        out_shape.append(jax.ShapeDtypeStruct((bsz, n, d), F32))
        scratch = [pltpu.VMEM((n + 2 * CONV_HALO, D_BRANCH), F32)] * 2 \
            + [pltpu.VMEM((n, D_BRANCH), BF16)] * 5 + [pltpu.VMEM((n, N_BRANCH * d), BF16)]
    return pl.pallas_call(
        functools.partial(_ctx_kernel, update),
        grid=(bsz,),
        in_specs=[tok(d)] + [_const_spec(a.shape) for a in consts],
        out_specs=out_specs,
        out_shape=out_shape,
        scratch_shapes=scratch,
        compiler_params=pltpu.CompilerParams(
            dimension_semantics=("parallel",), vmem_limit_bytes=VMEM_LIMIT),
        name="ctx_layer_update" if update else "ctx_layer_kv",
    )(xc, *consts)


def _bias_index_tables():
    qc = np.arange(GRID_W)[:, None]
    kc = np.arange(GRID_W)[None, :]
    col_start = np.clip(qc - WIN_COLS // 2, 0, GRID_W - WIN_COLS)
    valid = (kc - col_start >= 0) & (kc - col_start < WIN_COLS)
    dcol = np.clip(kc - qc, -(WIN_COLS - 1), WIN_COLS - 1) + WIN_COLS - 1
    return valid, dcol


def _bias_table(rpb_l):
    valid, dcol = _bias_index_tables()
    a = jnp.where(valid[None, None], rpb_l[:, :, dcol], NEG_INF)
    pair = jnp.concatenate([a[:, :N_BIAS_IDX], a[:, 1:N_BIAS_IDX + 1]], axis=-1)
    pair = pair.reshape(N_HGROUPS, HEADS_PER_GROUP, N_BIAS_IDX, GRID_W, 2 * GRID_W)
    return jnp.transpose(pair, (0, 2, 1, 3, 4)).reshape(
        N_HGROUPS, N_BIAS_IDX, HEADS_PER_GROUP * GRID_W, 2 * GRID_W).astype(F32)


def _pool_matrix():
    head = np.arange(D_BRANCH) // HEAD_DIM
    return jnp.asarray((head[:, None] == head[None, :]).astype(np.float32) / HEAD_DIM, dtype=BF16)


def kernel(x, c, ctx, c_ctx, norm_g, w_ada, b_ada, w_in, q_norm_g, k_norm_g, rpb, conv_short_w,
           conv_conf_w, conv_conf_b, ln_conf_g, ln_conf_b, w_out_a, w_out_b, w_out_c, w_o):
    depth = w_in.shape[0]
    bsz, seq, d = x.shape
    assert (d, seq % TM_MIX, seq % TM_PROJ) == (D_MODEL, 0, 0)

    n_cond = 8
    cond = jnp.zeros((n_cond, d), F32).at[:bsz].set(c).at[bsz].set(c_ctx)
    mods = _ada_call(cond, w_ada, b_ada)
    pool = _pool_matrix()
    rep = N_HEADS

    xc = ctx
    for l in range(depth):
        shift, scale, gate = (mods[l, :, j * d:(j + 1) * d] for j in range(3))
        lat = lambda t: t[:bsz].reshape(bsz, 1, d)
        cx = lambda t: t[bsz:bsz + 1]
        g_l = norm_g[l].reshape(1, d)
        w_l = w_in[l].astype(BF16)
        qg = jnp.tile(q_norm_g[l], rep).reshape(1, D_BRANCH)
        kg = jnp.tile(k_norm_g[l], rep).reshape(1, D_BRANCH)
        tail = (conv_short_w[l], conv_conf_w[l], conv_conf_b[l].reshape(1, D_BRANCH),
                ln_conf_g[l].reshape(1, D_BRANCH), ln_conf_b[l].reshape(1, D_BRANCH),
                w_out_a[l].astype(BF16), w_out_b[l].astype(BF16), w_out_c[l].astype(BF16),
                w_o[l].astype(BF16))

        update = l < depth - 1
        ctx_out = _ctx_call(update, xc, g_l, cx(scale), cx(shift), cx(gate), w_l, pool, qg, kg, *tail)
        kc, vc = ctx_out[0], ctx_out[1]

        q, k, v, sa, yb, yc, gates = _inproj_call(
            x, g_l, lat(scale), lat(shift), w_l, pool, qg, kg, *tail[:5])
        x = _mixer_call(x, q, k, v, kc, vc, _bias_table(rpb[l]), sa, yb, yc, gates, lat(gate), *tail[5:])
        if update:
            xc = ctx_out[2]
    return x
```

```python
import functools

import numpy as np
import jax
import jax.numpy as jnp
from jax import lax
from jax.experimental import pallas as pl
from jax.experimental.pallas import tpu as pltpu

F32 = jnp.float32
BF16 = jnp.bfloat16

D_MODEL = 1024
D_BRANCH = 512
HEAD_DIM = 64
N_HEADS = 8
GRID_W = 64
WIN_ROWS = 8
WIN_COLS = 16
SHORT_W = 3
CONF_W = 31
N_BRANCH = 3
N_PROJ_GROUPS = 11 + 2 * N_BRANCH
EPS = 1e-6
NEG_INF = -1e30

HEADS_PER_GROUP = 4
GROUP_W = HEADS_PER_GROUP * HEAD_DIM
N_HGROUPS = N_HEADS // HEADS_PER_GROUP
N_BIAS_IDX = 2 * WIN_ROWS - 2
CONV_HALO = 16

TM_PROJ = 512
ROWS_MIX = 8
TM_MIX = ROWS_MIX * GRID_W
KV_HALO = 4 * GRID_W
CONV_CHUNK = 64
LANES = 128
SUBLANES = 8

VMEM_LIMIT = 56 * 1024 * 1024


def _silu(x):
    return x * jax.nn.sigmoid(x)


def _modulate(x, g, scale, shift):
    y = x * lax.rsqrt(jnp.mean(x * x, axis=-1, keepdims=True) + EPS)
    return ((y * g) * (1.0 + scale) + shift).astype(BF16)


def _head_rms(z, g):
    low = lax.broadcasted_iota(jnp.int32, (1, LANES), 1) < HEAD_DIM
    outs = []
    for c0 in range(0, D_BRANCH, LANES):
        blk = z[:, c0:c0 + LANES]
        sq = blk * blk
        s_lo = jnp.sum(jnp.where(low, sq, 0.0), axis=-1, keepdims=True)
        s_hi = jnp.sum(jnp.where(low, 0.0, sq), axis=-1, keepdims=True)
        ms = jnp.where(low, s_lo, s_hi) * (1.0 / HEAD_DIM)
        outs.append(blk * lax.rsqrt(ms + EPS) * g[:, c0:c0 + LANES])
    return jnp.concatenate(outs, axis=-1)


def _proj_col(h, w_ref, i):
    return jnp.dot(h, w_ref[:, i * D_BRANCH:(i + 1) * D_BRANCH], preferred_element_type=F32)


def _head_masks():
    lane = lax.broadcasted_iota(jnp.int32, (1, GROUP_W), 1) // HEAD_DIM
    return [lane == h for h in range(HEADS_PER_GROUP)]


def _stack_heads(qg, masks):
    return jnp.concatenate([jnp.where(m, qg, jnp.zeros_like(qg)) for m in masks], axis=0)


def _unstack_heads(o, n, masks):
    out = jnp.where(masks[0], o[0:n], 0.0)
    for h in range(1, HEADS_PER_GROUP):
        out = out + jnp.where(masks[h], o[h * n:(h + 1) * n], 0.0)
    return out


def _dot_t(a, b):
    return lax.dot_general(a, b, (((1,), (1,)), ((), ())), preferred_element_type=F32)


def _ordering_zero(v):
    u = pltpu.bitcast(v, jnp.uint32)
    return pltpu.bitcast(lax.shift_right_logical(lax.shift_right_logical(u, jnp.uint32(16)), jnp.uint32(16)), F32)


def _dwconv(win_ref, w_ref, width, r0, n, c0, after=None):
    half = (width - 1) // 2
    lanes = slice(c0, c0 + LANES)
    offsets = [CONV_HALO + j - half for j in range(width)]
    span = n + SUBLANES * (max(offsets) // SUBLANES + 1)
    base = win_ref[r0:r0 + span, lanes]
    if after is not None:
        base = base + jnp.tile(after, (span // SUBLANES, 1))
    acc = None
    for s in range(SUBLANES):
        taps = [(j, o // SUBLANES) for j, o in enumerate(offsets) if o % SUBLANES == s]
        if not taps:
            continue
        shifted = pltpu.roll(base, span - s, 0) if s else base
        for j, q in taps:
            term = shifted[SUBLANES * q:SUBLANES * q + n] * w_ref[j:j + 1, lanes]
            acc = term if acc is None else acc + term
    return acc


def _layer_norm(u, g, b):
    mu = jnp.mean(u, axis=-1, keepdims=True)
    d = u - mu
    var = jnp.mean(d * d, axis=-1, keepdims=True)
    return d * lax.rsqrt(var + EPS) * g + b


def _short_conv_chunk(swin, sb_ref, wsc_ref, yb_ref, r0, n):
    rows = slice(r0, r0 + n)
    for c0 in range(0, D_BRANCH, LANES):
        yb = _dwconv(swin, wsc_ref, SHORT_W, r0, n, c0) * sb_ref[rows, c0:c0 + LANES].astype(F32)
        yb_ref[rows, c0:c0 + LANES] = yb.astype(BF16)


def _conf_conv_lanes(cwin, wcc_ref, r0, n, lane_blocks, after=None):
    return [_dwconv(cwin, wcc_ref, CONF_W, r0, n, b * LANES, after) for b in lane_blocks]


def _conf_finish(u_blocks, sc_ref, bcc_ref, lng_ref, lnb_ref, yc_ref, r0, n):
    rows = slice(r0, r0 + n)
    u = jnp.concatenate(u_blocks, axis=-1)
    u = _silu(_layer_norm(u + bcc_ref[...], lng_ref[...], lnb_ref[...]))
    yc = u * sc_ref[rows, :].astype(F32)
    yc_ref[rows, :] = yc.astype(BF16)
    return yc


def _conf_conv_chunk(cwin, sc_ref, wcc_ref, bcc_ref, lng_ref, lnb_ref, yc_ref, r0, n):
    u_blocks = _conf_conv_lanes(cwin, wcc_ref, r0, n, range(D_BRANCH // LANES))
    return _conf_finish(u_blocks, sc_ref, bcc_ref, lng_ref, lnb_ref, yc_ref, r0, n)


def _conv_branches(cwin, swin, sb_ref, sc_ref, wsc_ref, wcc_ref, bcc_ref, lng_ref, lnb_ref,
                   yb_ref, yc_ref, n_tokens):
    for r0 in range(0, n_tokens, CONV_CHUNK):
        n = min(CONV_CHUNK, n_tokens - r0)
        _short_conv_chunk(swin, sb_ref, wsc_ref, yb_ref, r0, n)
        _conf_conv_chunk(cwin, sc_ref, wcc_ref, bcc_ref, lng_ref, lnb_ref, yc_ref, r0, n)


def _merge(x, gate, ya, yb, yc, gates_ref, woa_ref, wob_ref, woc_ref, wo_ref):
    d = D_MODEL
    m = gates_ref[:, 0:d].astype(F32) * jnp.dot(ya, woa_ref[...], preferred_element_type=F32)
    m = m + gates_ref[:, d:2 * d].astype(F32) * jnp.dot(yb, wob_ref[...], preferred_element_type=F32)
    m = m + gates_ref[:, 2 * d:3 * d].astype(F32) * jnp.dot(yc, woc_ref[...], preferred_element_type=F32)
    return x + gate * jnp.dot(m.astype(BF16), wo_ref[...], preferred_element_type=F32)


def _ada_kernel(cond_ref, w_ref, b_ref, o_ref):
    o_ref[...] = jnp.dot(_silu(cond_ref[...]), w_ref[...], preferred_element_type=F32) + b_ref[...]


def _ada_call(cond, w_ada, b_ada):
    depth = w_ada.shape[0]
    n_rows = cond.shape[0]
    return pl.pallas_call(
        _ada_kernel,
        grid=(depth, 3),
        in_specs=[
            pl.BlockSpec((n_rows, D_MODEL), lambda l, j: (0, 0)),
            pl.BlockSpec((None, D_MODEL, D_MODEL), lambda l, j: (l, 0, j)),
            pl.BlockSpec((None, 1, D_MODEL), lambda l, j: (l, 0, j)),
        ],
        out_specs=pl.BlockSpec((None, n_rows, D_MODEL), lambda l, j: (l, 0, j)),
        out_shape=jax.ShapeDtypeStruct((depth, n_rows, 3 * D_MODEL), F32),
        name="ada_mod",
    )(cond, w_ada, b_ada.reshape(depth, 1, 3 * D_MODEL))


def _inproj_kernel(xp_ref, x_ref, xn_ref, g_ref, scale_ref, shift_ref, w_ref, qg_ref, kg_ref,
                   wsc_ref, wcc_ref, bcc_ref, lng_ref, lnb_ref,
                   q_o, k_o, v_o, sa_o, yb_o, yc_o, gates_o, h_s, cwin, swin, sb_s, sc_s):
    i = pl.program_id(1)
    mod = (g_ref[...], scale_ref[...], shift_ref[...])
    tile = slice(CONV_HALO, CONV_HALO + TM_PROJ)
    h_s[0:CONV_HALO, :] = _modulate(xp_ref[...], *mod)
    h_s[tile, :] = _modulate(x_ref[...], *mod)
    h_s[CONV_HALO + TM_PROJ:, :] = _modulate(xn_ref[...], *mod)
    col = lambda c: _proj_col(h_s[tile, :], w_ref, c)
    col_ext = lambda c: _proj_col(h_s[...], w_ref, c)

    has_prev = (i > 0).astype(F32)
    has_next = (i < pl.num_programs(1) - 1).astype(F32)

    def fill_window(win, val):
        win[0:CONV_HALO, :] = val[0:CONV_HALO] * has_prev
        win[tile, :] = val[tile]
        win[CONV_HALO + TM_PROJ:, :] = val[CONV_HALO + TM_PROJ:] * has_next

    started = lambda z: _ordering_zero(z[0:SUBLANES, 0:LANES])
    probe = (slice(CONV_HALO, 2 * CONV_HALO), slice(0, LANES))
    n_lane_blocks = D_BRANCH // LANES
    conv_parts, done = {}, {}

    def conf_half(hf, after):
        c, second = divmod(hf, 2)
        r0 = c * CONV_CHUNK
        blocks = range(n_lane_blocks // 2, n_lane_blocks) if second else range(n_lane_blocks // 2)
        u = _conf_conv_lanes(cwin, wcc_ref, r0, CONV_CHUNK, blocks, after)
        if second:
            yc = _conf_finish(conv_parts.pop(c) + u, sc_s, bcc_ref, lng_ref, lnb_ref, yc_o, r0, CONV_CHUNK)
            done[hf] = yc[0:CONV_HALO, 0:LANES]
        else:
            conv_parts[c] = u
            done[hf] = u[0][0:CONV_HALO]

    def wait_half(hf):
        h_s[probe] = h_s[probe] + _ordering_zero(done.pop(hf)).astype(BF16)

    def short_chunks(*cs):
        for c in cs:
            _short_conv_chunk(swin, sb_s, wsc_ref, yb_o, c * CONV_CHUNK, CONV_CHUNK)

    pending = {}

    def group_cv_a():
        pending["z5"] = col_ext(5)
        return pending["z5"]

    def group_cv_b():
        z6 = col_ext(6)
        fill_window(swin, pending.pop("z5") * z6)
        return z6

    def group_sb_a():
        pending["z4"] = col(4)
        return pending["z4"]

    def group_sb_b():
        z7 = col(7)
        sb_s[...] = pending.pop("z4") * _silu(z7)
        return z7

    def group_q():
        z = col(0)
        q_o[...] = (_head_rms(z, qg_ref[...]) * HEAD_DIM ** -0.5).astype(BF16)
        return z

    def group_k():
        z = col(1)
        k_o[...] = _head_rms(z, kg_ref[...]).astype(BF16)
        return z

    def group_v():
        z = col(2)
        v_o[...] = z.astype(BF16)
        return z

    def group_sa():
        z = col(3)
        sa_o[...] = _silu(z).astype(BF16)
        return z

    def gate_group(j):
        z = col(11 + j)
        gates_o[:, j * D_BRANCH:(j + 1) * D_BRANCH] = jax.nn.sigmoid(z).astype(BF16)
        return z

    groups = [group_cv_a, group_cv_b, group_sb_a, group_sb_b, group_q, group_k, group_v, group_sa]
    groups += [functools.partial(gate_group, j) for j in range(2 * N_BRANCH)]
    waits = {k: [k - 1] for k in range(1, 11)}
    waits.update({11: [10, 11], 12: [12, 13], 13: [14]})
    anchored = {k: [k + 1] for k in range(0, 9)}
    anchored.update({9: [10, 11], 10: [12, 13], 11: [14], 12: [15]})
    shorts = {5: (0, 1), 7: (2, 3), 9: (4, 5), 11: (6, 7)}

    fill_window(cwin, col_ext(8) * jax.nn.sigmoid(col_ext(9)))
    z10 = col(10)
    sc_s[...] = _silu(z10)
    conf_half(0, started(z10))
    for k, group in enumerate(groups):
        for hf in waits.get(k, ()):
            wait_half(hf)
        z = group()
        for hf in anchored.get(k, ()):
            conf_half(hf, started(z))
        short_chunks(*shorts.get(k, ()))


def _layer_spec(arr, l):
    nd = arr.ndim
    return pl.BlockSpec((None,) + arr.shape[1:], lambda *_: (l,) + (0,) * (nd - 1),
                        pipeline_mode=pl.Buffered(1))


def _mod_spec(mods, l, part, row=None):
    d = mods.shape[-1]
    if row is None:
        return pl.BlockSpec((None, None, None, 1, d), lambda b, *_: (l, b, part, 0, 0))
    return pl.BlockSpec((None, None, None, 1, d), lambda *_: (l, row, part, 0, 0))


def _inproj_call(l, x, mods, p):
    bsz, seq, d = x.shape
    halo_per_tile = TM_PROJ // CONV_HALO
    n_halo_blocks = seq // CONV_HALO
    tok = lambda w: pl.BlockSpec((None, TM_PROJ, w), lambda b, i: (b, i, 0))
    x_prev = pl.BlockSpec((None, CONV_HALO, d), lambda b, i: (b, jnp.maximum(i * halo_per_tile - 1, 0), 0))
    x_next = pl.BlockSpec((None, CONV_HALO, d),
                          lambda b, i: (b, jnp.minimum((i + 1) * halo_per_tile, n_halo_blocks - 1), 0))
    names = ("qg", "kg", "wsc", "wcc", "bcc", "lng", "lnb")
    widths = [D_BRANCH] * 6 + [N_BRANCH * D_MODEL]
    return pl.pallas_call(
        _inproj_kernel,
        grid=(bsz, seq // TM_PROJ),
        in_specs=[x_prev, tok(d), x_next, _layer_spec(p["norm_g"], l), _mod_spec(mods, l, 1), _mod_spec(mods, l, 0),
                  _layer_spec(p["w_in"], l)] + [_layer_spec(p[n], l) for n in names],
        out_specs=[tok(w) for w in widths],
        out_shape=[jax.ShapeDtypeStruct((bsz, seq, w), BF16) for w in widths],
        scratch_shapes=[pltpu.VMEM((TM_PROJ + 2 * CONV_HALO, d), BF16)]
        + [pltpu.VMEM((TM_PROJ + 2 * CONV_HALO, D_BRANCH), F32)] * 2
        + [pltpu.VMEM((TM_PROJ, D_BRANCH), F32)] * 2,
        compiler_params=pltpu.CompilerParams(
            dimension_semantics=("parallel", "parallel"), vmem_limit_bytes=VMEM_LIMIT),
        name="in_proj",
    )(x, x, x, p["norm_g"], mods, mods, p["w_in"], *(p[n] for n in names))


def _mixer_kernel(x_ref, q_ref, kp_ref, k_ref, kn_ref, vp_ref, v_ref, vn_ref, kc_ref, vc_ref,
                  bias_ref, sa_ref, yb_ref, yc_ref, gates_ref, gate_ref,
                  woa_ref, wob_ref, woc_ref, wo_ref, out_ref, kwin, vwin, ya_ref):
    i = pl.program_id(1)
    n_rows = pl.num_programs(1) * ROWS_MIX

    kwin[0:KV_HALO, :] = kp_ref[...]
    kwin[KV_HALO:KV_HALO + TM_MIX, :] = k_ref[...]
    kwin[KV_HALO + TM_MIX:, :] = kn_ref[...]
    vwin[0:KV_HALO, :] = vp_ref[...]
    vwin[KV_HALO:KV_HALO + TM_MIX, :] = v_ref[...]
    vwin[KV_HALO + TM_MIX:, :] = vn_ref[...]

    masks = _head_masks()
    tile_row0 = i * ROWS_MIX

    def attend_row(rr, carry):
        row = tile_row0 + rr
        row_start = jnp.clip(row - WIN_ROWS // 2, 0, n_rows - WIN_ROWS)
        delta = row - row_start
        k0 = pl.multiple_of((row_start - tile_row0 + 4) * GRID_W, GRID_W)
        q0 = pl.multiple_of(rr * GRID_W, GRID_W)
        q_row = q_ref[pl.ds(q0, GRID_W), :]
        outs = []
        for g in range(N_HGROUPS):
            lanes = slice(g * GROUP_W, (g + 1) * GROUP_W)
            qm = _stack_heads(q_row[:, lanes], masks)
            s_loc = _dot_t(qm, kwin[pl.ds(k0, WIN_ROWS * GRID_W), lanes])
            s_ctx = _dot_t(qm, kc_ref[:, lanes])
            bias = jnp.concatenate(
                [bias_ref[g, 2 * j + WIN_ROWS - 1 - delta] for j in range(WIN_ROWS // 2)], axis=-1)
            s_loc = s_loc + bias
            m = jnp.maximum(jnp.max(s_loc, axis=-1, keepdims=True), jnp.max(s_ctx, axis=-1, keepdims=True))
            p_loc = jnp.exp(s_loc - m)
            p_ctx = jnp.exp(s_ctx - m)
            denom = jnp.sum(p_loc, axis=-1, keepdims=True) + jnp.sum(p_ctx, axis=-1, keepdims=True)
            o = jnp.dot(p_loc.astype(BF16), vwin[pl.ds(k0, WIN_ROWS * GRID_W), lanes],
                        preferred_element_type=F32)
            o = o + jnp.dot(p_ctx.astype(BF16), vc_ref[:, lanes], preferred_element_type=F32)
            outs.append(_unstack_heads(o * (1.0 / denom), GRID_W, masks))
        att = jnp.concatenate(outs, axis=-1)
        ya_ref[pl.ds(q0, GRID_W), :] = (att * sa_ref[pl.ds(q0, GRID_W), :].astype(F32)).astype(BF16)
        return carry

    lax.fori_loop(0, ROWS_MIX, attend_row, 0, unroll=True)

    out_ref[...] = _merge(x_ref[...], gate_ref[...], ya_ref[...], yb_ref[...], yc_ref[...],
                          gates_ref, woa_ref, wob_ref, woc_ref, wo_ref)


def _mixer_call(l, x, q, k, v, kc, vc, bias, sa, yb, yc, gates, mods, p):
    bsz, seq, d = x.shape
    n_kv_blocks = seq // KV_HALO
    kv_per_tile = TM_MIX // KV_HALO
    ctx_len = kc.shape[1]

    tok = lambda w: pl.BlockSpec((None, TM_MIX, w), lambda b, i: (b, i, 0))
    kv_prev = pl.BlockSpec((None, KV_HALO, D_BRANCH),
                           lambda b, i: (b, jnp.maximum(i * kv_per_tile - 1, 0), 0))
    kv_next = pl.BlockSpec((None, KV_HALO, D_BRANCH),
                           lambda b, i: (b, jnp.minimum((i + 1) * kv_per_tile, n_kv_blocks - 1), 0))
    ctx_spec = pl.BlockSpec((None, ctx_len, D_BRANCH), lambda b, i: (b, 0, 0))
    tb = tok(D_BRANCH)
    names = ("woa", "wob", "woc", "wo")

    in_specs = [tok(d), tb, kv_prev, tb, kv_next, kv_prev, tb, kv_next, ctx_spec, ctx_spec,
                _layer_spec(bias, l), tb, tb, tb, tok(N_BRANCH * d), _mod_spec(mods, l, 2)] \
        + [_layer_spec(p[n], l) for n in names]
    return pl.pallas_call(
        _mixer_kernel,
        grid=(bsz, seq // TM_MIX),
        in_specs=in_specs,
        out_specs=tok(d),
        out_shape=jax.ShapeDtypeStruct((bsz, seq, d), F32),
        scratch_shapes=[
            pltpu.VMEM((TM_MIX + 2 * KV_HALO, D_BRANCH), BF16),
            pltpu.VMEM((TM_MIX + 2 * KV_HALO, D_BRANCH), BF16),
            pltpu.VMEM((TM_MIX, D_BRANCH), BF16),
        ],
        compiler_params=pltpu.CompilerParams(
            dimension_semantics=("parallel", "parallel"), vmem_limit_bytes=VMEM_LIMIT),
        name="mixer",
    )(x, q, k, k, k, v, v, v, kc, vc, bias, sa, yb, yc, gates, mods, *(p[n] for n in names))


def _ctx_kernel(update, xc_ref, g_ref, scale_ref, shift_ref, gate_ref, w_ref, qg_ref, kg_ref,
                wsc_ref, wcc_ref, bcc_ref, lng_ref, lnb_ref, woa_ref, wob_ref, woc_ref, wo_ref,
                *refs):
    if update:
        kc_o, vc_o, xc_o, cwin, swin, sb_s, sc_s, ya_s, yb_s, yc_s, gates_s = refs
    else:
        kc_o, vc_o = refs
    n = xc_ref.shape[0]
    xc = xc_ref[...]
    h = _modulate(xc, g_ref[...], scale_ref[...], shift_ref[...])
    kc = _head_rms(_proj_col(h, w_ref, 1), kg_ref[...]).astype(BF16)
    vc = _proj_col(h, w_ref, 2).astype(BF16)
    kc_o[...] = kc
    vc_o[...] = vc
    if not update:
        return

    q = (_head_rms(_proj_col(h, w_ref, 0), qg_ref[...]) * HEAD_DIM ** -0.5).astype(BF16)
    sa = _silu(_proj_col(h, w_ref, 3))
    masks = _head_masks()
    outs = []
    for g in range(N_HGROUPS):
        lanes = slice(g * GROUP_W, (g + 1) * GROUP_W)
        qm = _stack_heads(q[:, lanes], masks)
        s = _dot_t(qm, kc[:, lanes])
        p = jnp.exp(s - jnp.max(s, axis=-1, keepdims=True))
        o = jnp.dot(p.astype(BF16), vc[:, lanes], preferred_element_type=F32)
        outs.append(_unstack_heads(o * (1.0 / jnp.sum(p, axis=-1, keepdims=True)), n, masks))
    ya_s[...] = (jnp.concatenate(outs, axis=-1) * sa).astype(BF16)

    sb_s[...] = (_proj_col(h, w_ref, 4) * _silu(_proj_col(h, w_ref, 7))).astype(BF16)
    sc_s[...] = _silu(_proj_col(h, w_ref, 10)).astype(BF16)
    zeros = jnp.zeros((CONV_HALO, D_BRANCH), F32)
    for win, val in ((swin, _proj_col(h, w_ref, 5) * _proj_col(h, w_ref, 6)),
                     (cwin, _proj_col(h, w_ref, 8) * jax.nn.sigmoid(_proj_col(h, w_ref, 9)))):
        win[0:CONV_HALO, :] = zeros
        win[CONV_HALO:CONV_HALO + n, :] = val
        win[CONV_HALO + n:, :] = zeros
    _conv_branches(cwin, swin, sb_s, sc_s, wsc_ref, wcc_ref, bcc_ref, lng_ref, lnb_ref, yb_s, yc_s, n)
    for i in range(2 * N_BRANCH):
        gates_s[:, i * D_BRANCH:(i + 1) * D_BRANCH] = jax.nn.sigmoid(_proj_col(h, w_ref, 11 + i)).astype(BF16)
    xc_o[...] = _merge(xc, gate_ref[...], ya_s[...], yb_s[...], yc_s[...],
                       gates_s, woa_ref, wob_ref, woc_ref, wo_ref)


def _ctx_call(l, update, xc, mods, ctx_row, p):
    bsz, n, d = xc.shape
    tok = lambda w: pl.BlockSpec((None, n, w), lambda b: (b, 0, 0))
    names = ("qg", "kg", "wsc", "wcc", "bcc", "lng", "lnb", "woa", "wob", "woc", "wo")
    out_specs = [tok(D_BRANCH), tok(D_BRANCH)]
    out_shape = [jax.ShapeDtypeStruct((bsz, n, D_BRANCH), BF16)] * 2
    scratch = []
    if update:
        out_specs.append(tok(d))
        out_shape.append(jax.ShapeDtypeStruct((bsz, n, d), F32))
        scratch = [pltpu.VMEM((n + 2 * CONV_HALO, D_BRANCH), F32)] * 2 \
            + [pltpu.VMEM((n, D_BRANCH), BF16)] * 5 + [pltpu.VMEM((n, N_BRANCH * d), BF16)]
    return pl.pallas_call(
        functools.partial(_ctx_kernel, update),
        grid=(bsz,),
        in_specs=[tok(d), _layer_spec(p["norm_g"], l), _mod_spec(mods, l, 1, ctx_row), _mod_spec(mods, l, 0, ctx_row),
                  _mod_spec(mods, l, 2, ctx_row), _layer_spec(p["w_in"], l)]
        + [_layer_spec(p[n], l) for n in names],
        out_specs=out_specs,
        out_shape=out_shape,
        scratch_shapes=scratch,
        compiler_params=pltpu.CompilerParams(
            dimension_semantics=("parallel",), vmem_limit_bytes=VMEM_LIMIT),
        name="ctx_layer_update" if update else "ctx_layer_kv",
    )(xc, p["norm_g"], mods, mods, mods, p["w_in"], *(p[n] for n in names))


def _bias_tables(rpb):
    depth, n_heads, n_drow, n_dcol = rpb.shape
    qc = np.arange(GRID_W)[:, None]
    kc = np.arange(GRID_W)[None, :]
    col_start = np.clip(qc - WIN_COLS // 2, 0, GRID_W - WIN_COLS)
    valid = (kc - col_start >= 0) & (kc - col_start < WIN_COLS)
    period = 2 * GRID_W
    lead = GRID_W - WIN_COLS
    e = jnp.pad(rpb, ((0, 0), (0, 0), (0, 0), (lead, period - lead - n_dcol)))
    t = jnp.tile(e, (1, 1, 1, GRID_W))[..., :GRID_W * (period - 1)]
    t = t.reshape(depth, n_heads, n_drow, GRID_W, period - 1)[..., GRID_W - 1:]
    a = jnp.where(valid, t, NEG_INF)
    pair = jnp.concatenate([a[:, :, :N_BIAS_IDX], a[:, :, 1:N_BIAS_IDX + 1]], axis=-1)
    pair = pair.reshape(depth, N_HGROUPS, HEADS_PER_GROUP, N_BIAS_IDX, GRID_W, 2 * GRID_W)
    return jnp.transpose(pair, (0, 1, 3, 2, 4, 5)).reshape(
        depth, N_HGROUPS, N_BIAS_IDX, HEADS_PER_GROUP * GRID_W, 2 * GRID_W)


def kernel(x, c, ctx, c_ctx, norm_g, w_ada, b_ada, w_in, q_norm_g, k_norm_g, rpb, conv_short_w,
           conv_conf_w, conv_conf_b, ln_conf_g, ln_conf_b, w_out_a, w_out_b, w_out_c, w_o):
    depth = w_in.shape[0]
    bsz, seq, d = x.shape
    assert (d, seq % TM_MIX, seq % TM_PROJ) == (D_MODEL, 0, 0)

    n_cond = 8
    cond = jnp.concatenate([c, c_ctx[None], jnp.zeros((n_cond - bsz - 1, d), F32)], axis=0)
    mods = _ada_call(cond, w_ada, b_ada).reshape(depth, n_cond, 3, 1, d)
    row = lambda t: t.reshape(depth, 1, t.shape[-1])
    p = {
        "norm_g": row(norm_g), "w_in": w_in.astype(BF16),
        "qg": row(jnp.tile(q_norm_g, (1, N_HEADS))), "kg": row(jnp.tile(k_norm_g, (1, N_HEADS))),
        "wsc": conv_short_w, "wcc": conv_conf_w, "bcc": row(conv_conf_b),
        "lng": row(ln_conf_g), "lnb": row(ln_conf_b),
        "woa": w_out_a.astype(BF16), "wob": w_out_b.astype(BF16), "woc": w_out_c.astype(BF16),
        "wo": w_o.astype(BF16),
    }
    bias = _bias_tables(rpb)

    xc = ctx
    for l in range(depth):
        update = l < depth - 1
        ctx_out = _ctx_call(l, update, xc, mods, bsz, p)
        kc, vc = ctx_out[0], ctx_out[1]
        q, k, v, sa, yb, yc, gates = _inproj_call(l, x, mods, p)
        x = _mixer_call(l, x, q, k, v, kc, vc, bias, sa, yb, yc, gates, mods, p)
        if update:
            xc = ctx_out[2]
    return x
```

```python
import functools

import numpy as np
import jax
import jax.numpy as jnp
from jax import lax
from jax.experimental import pallas as pl
from jax.experimental.pallas import tpu as pltpu

F32 = jnp.float32
BF16 = jnp.bfloat16

D_MODEL = 1024
D_BRANCH = 512
HEAD_DIM = 64
N_HEADS = 8
GRID_W = 64
WIN_ROWS = 8
WIN_COLS = 16
SHORT_W = 3
CONF_W = 31
N_BRANCH = 3
N_PROJ_GROUPS = 11 + 2 * N_BRANCH
EPS = 1e-6
NEG_INF = -1e30

HEADS_PER_GROUP = 4
GROUP_W = HEADS_PER_GROUP * HEAD_DIM
N_HGROUPS = N_HEADS // HEADS_PER_GROUP
N_BIAS_IDX = 2 * WIN_ROWS - 2
CONV_HALO = 16

TM_PROJ = 512
ROWS_MIX = 8
TM_MIX = ROWS_MIX * GRID_W
KV_HALO = 4 * GRID_W
CONV_CHUNK = 64
LANES = 128
SUBLANES = 8

VMEM_LIMIT = 56 * 1024 * 1024


def _silu(x):
    return x * jax.nn.sigmoid(x)


def _modulate(x, g, scale, shift):
    y = x * lax.rsqrt(jnp.mean(x * x, axis=-1, keepdims=True) + EPS)
    return ((y * g) * (1.0 + scale) + shift).astype(BF16)


def _head_rms(z, g):
    low = lax.broadcasted_iota(jnp.int32, (1, LANES), 1) < HEAD_DIM
    outs = []
    for c0 in range(0, D_BRANCH, LANES):
        blk = z[:, c0:c0 + LANES]
        sq = blk * blk
        s_lo = jnp.sum(jnp.where(low, sq, 0.0), axis=-1, keepdims=True)
        s_hi = jnp.sum(jnp.where(low, 0.0, sq), axis=-1, keepdims=True)
        ms = jnp.where(low, s_lo, s_hi) * (1.0 / HEAD_DIM)
        outs.append(blk * lax.rsqrt(ms + EPS) * g[:, c0:c0 + LANES])
    return jnp.concatenate(outs, axis=-1)


def _proj_col(h, w_ref, i):
    return jnp.dot(h, w_ref[:, i * D_BRANCH:(i + 1) * D_BRANCH], preferred_element_type=F32)


def _head_masks():
    lane = lax.broadcasted_iota(jnp.int32, (1, GROUP_W), 1) // HEAD_DIM
    return [lane == h for h in range(HEADS_PER_GROUP)]


def _stack_heads(qg, masks):
    return jnp.concatenate([jnp.where(m, qg, jnp.zeros_like(qg)) for m in masks], axis=0)


def _unstack_heads(o, n, masks):
    out = jnp.where(masks[0], o[0:n], 0.0)
    for h in range(1, HEADS_PER_GROUP):
        out = out + jnp.where(masks[h], o[h * n:(h + 1) * n], 0.0)
    return out


def _dot_t(a, b):
    return lax.dot_general(a, b, (((1,), (1,)), ((), ())), preferred_element_type=F32)


def _ordering_zero(v):
    u = pltpu.bitcast(v, jnp.uint32)
    return pltpu.bitcast(lax.shift_right_logical(lax.shift_right_logical(u, jnp.uint32(16)), jnp.uint32(16)), F32)


def _dwconv(win_ref, w_ref, width, r0, n, c0, after=None):
    half = (width - 1) // 2
    lanes = slice(c0, c0 + LANES)
    offsets = [CONV_HALO + j - half for j in range(width)]
    span = n + SUBLANES * (max(offsets) // SUBLANES + 1)
    base = win_ref[r0:r0 + span, lanes]
    if after is not None:
        base = base + jnp.tile(after, (span // SUBLANES, 1))
    acc = None
    for s in range(SUBLANES):
        taps = [(j, o // SUBLANES) for j, o in enumerate(offsets) if o % SUBLANES == s]
        if not taps:
            continue
        shifted = pltpu.roll(base, span - s, 0) if s else base
        for j, q in taps:
            term = shifted[SUBLANES * q:SUBLANES * q + n] * w_ref[j:j + 1, lanes]
            acc = term if acc is None else acc + term
    return acc


def _layer_norm(u, g, b):
    mu = jnp.mean(u, axis=-1, keepdims=True)
    d = u - mu
    var = jnp.mean(d * d, axis=-1, keepdims=True)
    return d * lax.rsqrt(var + EPS) * g + b


def _short_conv_chunk(swin, sb_ref, wsc_ref, yb_ref, r0, n):
    rows = slice(r0, r0 + n)
    for c0 in range(0, D_BRANCH, LANES):
        yb = _dwconv(swin, wsc_ref, SHORT_W, r0, n, c0) * sb_ref[rows, c0:c0 + LANES].astype(F32)
        yb_ref[rows, c0:c0 + LANES] = yb.astype(BF16)


def _conf_conv_lanes(cwin, wcc_ref, r0, n, lane_blocks, after=None):
    return [_dwconv(cwin, wcc_ref, CONF_W, r0, n, b * LANES, after) for b in lane_blocks]


def _conf_finish(u_blocks, sc_ref, bcc_ref, lng_ref, lnb_ref, yc_ref, r0, n):
    rows = slice(r0, r0 + n)
    u = jnp.concatenate(u_blocks, axis=-1)
    u = _silu(_layer_norm(u + bcc_ref[...], lng_ref[...], lnb_ref[...]))
    yc = u * sc_ref[rows, :].astype(F32)
    yc_ref[rows, :] = yc.astype(BF16)
    return yc


def _conf_conv_chunk(cwin, sc_ref, wcc_ref, bcc_ref, lng_ref, lnb_ref, yc_ref, r0, n):
    u_blocks = _conf_conv_lanes(cwin, wcc_ref, r0, n, range(D_BRANCH // LANES))
    return _conf_finish(u_blocks, sc_ref, bcc_ref, lng_ref, lnb_ref, yc_ref, r0, n)


def _conv_branches(cwin, swin, sb_ref, sc_ref, wsc_ref, wcc_ref, bcc_ref, lng_ref, lnb_ref,
                   yb_ref, yc_ref, n_tokens):
    for r0 in range(0, n_tokens, CONV_CHUNK):
        n = min(CONV_CHUNK, n_tokens - r0)
        _short_conv_chunk(swin, sb_ref, wsc_ref, yb_ref, r0, n)
        _conf_conv_chunk(cwin, sc_ref, wcc_ref, bcc_ref, lng_ref, lnb_ref, yc_ref, r0, n)


def _merge(x, gate, ya, yb, yc, gates_ref, woa_ref, wob_ref, woc_ref, wo_ref):
    d = D_MODEL
    branch_gate = lambda j: jax.nn.sigmoid(gates_ref[:, j * d:(j + 1) * d].astype(F32))
    m = branch_gate(0) * jnp.dot(ya, woa_ref[...], preferred_element_type=F32)
    m = m + branch_gate(1) * jnp.dot(yb, wob_ref[...], preferred_element_type=F32)
    m = m + branch_gate(2) * jnp.dot(yc, woc_ref[...], preferred_element_type=F32)
    return x + gate * jnp.dot(m.astype(BF16), wo_ref[...], preferred_element_type=F32)


def _ada_kernel(cond_ref, w_ref, b_ref, o_ref):
    o_ref[...] = jnp.dot(_silu(cond_ref[...]), w_ref[...], preferred_element_type=F32) + b_ref[...]


def _ada_call(cond, w_ada, b_ada):
    depth = w_ada.shape[0]
    n_rows = cond.shape[0]
    return pl.pallas_call(
        _ada_kernel,
        grid=(depth, 3),
        in_specs=[
            pl.BlockSpec((n_rows, D_MODEL), lambda l, j: (0, 0)),
            pl.BlockSpec((None, D_MODEL, D_MODEL), lambda l, j: (l, 0, j)),
            pl.BlockSpec((None, 1, D_MODEL), lambda l, j: (l, 0, j)),
        ],
        out_specs=pl.BlockSpec((None, n_rows, D_MODEL), lambda l, j: (l, 0, j)),
        out_shape=jax.ShapeDtypeStruct((depth, n_rows, 3 * D_MODEL), F32),
        name="ada_mod",
    )(cond, w_ada, b_ada.reshape(depth, 1, 3 * D_MODEL))


def _inproj_kernel(xp_ref, x_ref, xn_ref, g_ref, scale_ref, shift_ref, w_ref, qg_ref, kg_ref,
                   wsc_ref, wcc_ref, bcc_ref, lng_ref, lnb_ref,
                   q_o, k_o, v_o, sa_o, yb_o, yc_o, gates_o, h_s, cwin, swin, sb_s, sc_s):
    i = pl.program_id(1)
    mod = (g_ref[...], scale_ref[...], shift_ref[...])
    tile = slice(CONV_HALO, CONV_HALO + TM_PROJ)
    h_s[0:CONV_HALO, :] = _modulate(xp_ref[...], *mod)
    h_s[tile, :] = _modulate(x_ref[...], *mod)
    h_s[CONV_HALO + TM_PROJ:, :] = _modulate(xn_ref[...], *mod)
    col = lambda c: _proj_col(h_s[tile, :], w_ref, c)
    col_ext = lambda c: _proj_col(h_s[...], w_ref, c)

    has_prev = (i > 0).astype(F32)
    has_next = (i < pl.num_programs(1) - 1).astype(F32)

    def fill_window(win, val):
        win[0:CONV_HALO, :] = val[0:CONV_HALO] * has_prev
        win[tile, :] = val[tile]
        win[CONV_HALO + TM_PROJ:, :] = val[CONV_HALO + TM_PROJ:] * has_next

    started = lambda z: _ordering_zero(z[0:SUBLANES, 0:LANES])
    probe = (slice(CONV_HALO, 2 * CONV_HALO), slice(0, LANES))
    n_lane_blocks = D_BRANCH // LANES
    conv_parts, done = {}, {}

    def conf_half(hf, after):
        c, second = divmod(hf, 2)
        r0 = c * CONV_CHUNK
        blocks = range(n_lane_blocks // 2, n_lane_blocks) if second else range(n_lane_blocks // 2)
        u = _conf_conv_lanes(cwin, wcc_ref, r0, CONV_CHUNK, blocks, after)
        if second:
            yc = _conf_finish(conv_parts.pop(c) + u, sc_s, bcc_ref, lng_ref, lnb_ref, yc_o, r0, CONV_CHUNK)
            done[hf] = yc[0:CONV_HALO, 0:LANES]
        else:
            conv_parts[c] = u
            done[hf] = u[0][0:CONV_HALO]

    def wait_half(hf):
        h_s[probe] = h_s[probe] + _ordering_zero(done.pop(hf)).astype(BF16)

    def short_chunks(*cs):
        for c in cs:
            _short_conv_chunk(swin, sb_s, wsc_ref, yb_o, c * CONV_CHUNK, CONV_CHUNK)

    pending = {}

    def group_cv_a():
        pending["z5"] = col_ext(5)
        return pending["z5"]

    def group_cv_b():
        z6 = col_ext(6)
        fill_window(swin, pending.pop("z5") * z6)
        return z6

    def group_sb_a():
        pending["z4"] = col(4)
        return pending["z4"]

    def group_sb_b():
        z7 = col(7)
        sb_s[...] = pending.pop("z4") * _silu(z7)
        return z7

    def group_q():
        z = col(0)
        q_o[...] = (_head_rms(z, qg_ref[...]) * HEAD_DIM ** -0.5).astype(BF16)
        return z

    def group_k():
        z = col(1)
        k_o[...] = _head_rms(z, kg_ref[...]).astype(BF16)
        return z

    def group_v():
        z = col(2)
        v_o[...] = z.astype(BF16)
        return z

    def group_sa():
        z = col(3)
        sa_o[...] = z.astype(BF16)
        return z

    def gate_group(j):
        z = col(11 + j)
        gates_o[:, j * D_BRANCH:(j + 1) * D_BRANCH] = z.astype(BF16)
        return z

    groups = [group_cv_a, group_cv_b, group_sb_a, group_sb_b, group_q, group_k, group_v, group_sa]
    groups += [functools.partial(gate_group, j) for j in range(2 * N_BRANCH)]
    waits = {k: [k - 1] for k in range(1, 11)}
    waits.update({11: [10, 11], 12: [12, 13], 13: [14]})
    anchored = {k: [k + 1] for k in range(0, 9)}
    anchored.update({9: [10, 11], 10: [12, 13], 11: [14], 12: [15]})
    shorts = {5: (0, 1), 7: (2, 3), 9: (4, 5), 11: (6, 7)}

    fill_window(cwin, col_ext(8) * jax.nn.sigmoid(col_ext(9)))
    z10 = col(10)
    sc_s[...] = _silu(z10)
    conf_half(0, started(z10))
    for k, group in enumerate(groups):
        for hf in waits.get(k, ()):
            wait_half(hf)
        z = group()
        for hf in anchored.get(k, ()):
            conf_half(hf, started(z))
        short_chunks(*shorts.get(k, ()))


def _layer_spec(arr, l):
    nd = arr.ndim
    return pl.BlockSpec((None,) + arr.shape[1:], lambda *_: (l,) + (0,) * (nd - 1),
                        pipeline_mode=pl.Buffered(1))


def _mod_spec(mods, l, part, row=None):
    d = mods.shape[-1]
    if row is None:
        return pl.BlockSpec((None, None, None, 1, d), lambda b, *_: (l, b, part, 0, 0))
    return pl.BlockSpec((None, None, None, 1, d), lambda *_: (l, row, part, 0, 0))


def _inproj_call(l, x, mods, p):
    bsz, seq, d = x.shape
    halo_per_tile = TM_PROJ // CONV_HALO
    n_halo_blocks = seq // CONV_HALO
    tok = lambda w: pl.BlockSpec((None, TM_PROJ, w), lambda b, i: (b, i, 0))
    x_prev = pl.BlockSpec((None, CONV_HALO, d), lambda b, i: (b, jnp.maximum(i * halo_per_tile - 1, 0), 0))
    x_next = pl.BlockSpec((None, CONV_HALO, d),
                          lambda b, i: (b, jnp.minimum((i + 1) * halo_per_tile, n_halo_blocks - 1), 0))
    names = ("qg", "kg", "wsc", "wcc", "bcc", "lng", "lnb")
    widths = [D_BRANCH] * 6 + [N_BRANCH * D_MODEL]
    return pl.pallas_call(
        _inproj_kernel,
        grid=(bsz, seq // TM_PROJ),
        in_specs=[x_prev, tok(d), x_next, _layer_spec(p["norm_g"], l), _mod_spec(mods, l, 1), _mod_spec(mods, l, 0),
                  _layer_spec(p["w_in"], l)] + [_layer_spec(p[n], l) for n in names],
        out_specs=[tok(w) for w in widths],
        out_shape=[jax.ShapeDtypeStruct((bsz, seq, w), BF16) for w in widths],
        scratch_shapes=[pltpu.VMEM((TM_PROJ + 2 * CONV_HALO, d), BF16)]
        + [pltpu.VMEM((TM_PROJ + 2 * CONV_HALO, D_BRANCH), F32)] * 2
        + [pltpu.VMEM((TM_PROJ, D_BRANCH), F32)] * 2,
        compiler_params=pltpu.CompilerParams(
            dimension_semantics=("parallel", "parallel"), vmem_limit_bytes=VMEM_LIMIT),
        name="in_proj",
    )(x, x, x, p["norm_g"], mods, mods, p["w_in"], *(p[n] for n in names))


def _mixer_kernel(x_ref, q_ref, kp_ref, k_ref, kn_ref, vp_ref, v_ref, vn_ref, kc_ref, vc_ref,
                  bias_ref, sa_ref, yb_ref, yc_ref, gates_ref, gate_ref,
                  woa_ref, wob_ref, woc_ref, wo_ref, out_ref, kwin, vwin, ya_ref):
    i = pl.program_id(1)
    n_rows = pl.num_programs(1) * ROWS_MIX

    kwin[0:KV_HALO, :] = kp_ref[...]
    kwin[KV_HALO:KV_HALO + TM_MIX, :] = k_ref[...]
    kwin[KV_HALO + TM_MIX:, :] = kn_ref[...]
    vwin[0:KV_HALO, :] = vp_ref[...]
    vwin[KV_HALO:KV_HALO + TM_MIX, :] = v_ref[...]
    vwin[KV_HALO + TM_MIX:, :] = vn_ref[...]

    masks = _head_masks()
    tile_row0 = i * ROWS_MIX

    def attend_row(rr, carry):
        row = tile_row0 + rr
        row_start = jnp.clip(row - WIN_ROWS // 2, 0, n_rows - WIN_ROWS)
        delta = row - row_start
        k0 = pl.multiple_of((row_start - tile_row0 + 4) * GRID_W, GRID_W)
        q0 = pl.multiple_of(rr * GRID_W, GRID_W)
        q_row = q_ref[pl.ds(q0, GRID_W), :]
        outs = []
        for g in range(N_HGROUPS):
            lanes = slice(g * GROUP_W, (g + 1) * GROUP_W)
            qm = _stack_heads(q_row[:, lanes], masks)
            s_loc = _dot_t(qm, kwin[pl.ds(k0, WIN_ROWS * GRID_W), lanes])
            s_ctx = _dot_t(qm, kc_ref[:, lanes])
            bias = jnp.concatenate(
                [bias_ref[g, 2 * j + WIN_ROWS - 1 - delta] for j in range(WIN_ROWS // 2)], axis=-1)
            s_loc = s_loc + bias
            m = jnp.maximum(jnp.max(s_loc, axis=-1, keepdims=True), jnp.max(s_ctx, axis=-1, keepdims=True))
            p_loc = jnp.exp(s_loc - m)
            p_ctx = jnp.exp(s_ctx - m)
            denom = jnp.sum(p_loc, axis=-1, keepdims=True) + jnp.sum(p_ctx, axis=-1, keepdims=True)
            o = jnp.dot(p_loc.astype(BF16), vwin[pl.ds(k0, WIN_ROWS * GRID_W), lanes],
                        preferred_element_type=F32)
            o = o + jnp.dot(p_ctx.astype(BF16), vc_ref[:, lanes], preferred_element_type=F32)
            outs.append(_unstack_heads(o * (1.0 / denom), GRID_W, masks))
        att = jnp.concatenate(outs, axis=-1)
        ya_ref[pl.ds(q0, GRID_W), :] = (att * _silu(sa_ref[pl.ds(q0, GRID_W), :].astype(F32))).astype(BF16)
        return carry

    lax.fori_loop(0, ROWS_MIX, attend_row, 0, unroll=True)

    out_ref[...] = _merge(x_ref[...], gate_ref[...], ya_ref[...], yb_ref[...], yc_ref[...],
                          gates_ref, woa_ref, wob_ref, woc_ref, wo_ref)


def _mixer_call(l, x, q, k, v, kc, vc, bias, sa, yb, yc, gates, mods, p):
    bsz, seq, d = x.shape
    n_kv_blocks = seq // KV_HALO
    kv_per_tile = TM_MIX // KV_HALO
    ctx_len = kc.shape[1]

    tok = lambda w: pl.BlockSpec((None, TM_MIX, w), lambda b, i: (b, i, 0))
    kv_prev = pl.BlockSpec((None, KV_HALO, D_BRANCH),
                           lambda b, i: (b, jnp.maximum(i * kv_per_tile - 1, 0), 0))
    kv_next = pl.BlockSpec((None, KV_HALO, D_BRANCH),
                           lambda b, i: (b, jnp.minimum((i + 1) * kv_per_tile, n_kv_blocks - 1), 0))
    ctx_spec = pl.BlockSpec((None, ctx_len, D_BRANCH), lambda b, i: (b, 0, 0))
    tb = tok(D_BRANCH)
    names = ("woa", "wob", "woc", "wo")

    in_specs = [tok(d), tb, kv_prev, tb, kv_next, kv_prev, tb, kv_next, ctx_spec, ctx_spec,
                _layer_spec(bias, l), tb, tb, tb, tok(N_BRANCH * d), _mod_spec(mods, l, 2)] \
        + [_layer_spec(p[n], l) for n in names]
    return pl.pallas_call(
        _mixer_kernel,
        grid=(bsz, seq // TM_MIX),
        in_specs=in_specs,
        out_specs=tok(d),
        out_shape=jax.ShapeDtypeStruct((bsz, seq, d), F32),
        scratch_shapes=[
            pltpu.VMEM((TM_MIX + 2 * KV_HALO, D_BRANCH), BF16),
            pltpu.VMEM((TM_MIX + 2 * KV_HALO, D_BRANCH), BF16),
            pltpu.VMEM((TM_MIX, D_BRANCH), BF16),
        ],
        compiler_params=pltpu.CompilerParams(
            dimension_semantics=("parallel", "parallel"), vmem_limit_bytes=VMEM_LIMIT),
        name="mixer",
    )(x, q, k, k, k, v, v, v, kc, vc, bias, sa, yb, yc, gates, mods, *(p[n] for n in names))


def _ctx_kernel(update, xc_ref, g_ref, scale_ref, shift_ref, gate_ref, w_ref, qg_ref, kg_ref,
                wsc_ref, wcc_ref, bcc_ref, lng_ref, lnb_ref, woa_ref, wob_ref, woc_ref, wo_ref,
                *refs):
    if update:
        kc_o, vc_o, xc_o, cwin, swin, sb_s, sc_s, ya_s, yb_s, yc_s, gates_s = refs
    else:
        kc_o, vc_o = refs
    n = xc_ref.shape[0]
    xc = xc_ref[...]
    h = _modulate(xc, g_ref[...], scale_ref[...], shift_ref[...])
    kc = _head_rms(_proj_col(h, w_ref, 1), kg_ref[...]).astype(BF16)
    vc = _proj_col(h, w_ref, 2).astype(BF16)
    kc_o[...] = kc
    vc_o[...] = vc
    if not update:
        return

    q = (_head_rms(_proj_col(h, w_ref, 0), qg_ref[...]) * HEAD_DIM ** -0.5).astype(BF16)
    sa = _silu(_proj_col(h, w_ref, 3))
    masks = _head_masks()
    outs = []
    for g in range(N_HGROUPS):
        lanes = slice(g * GROUP_W, (g + 1) * GROUP_W)
        qm = _stack_heads(q[:, lanes], masks)
        s = _dot_t(qm, kc[:, lanes])
        p = jnp.exp(s - jnp.max(s, axis=-1, keepdims=True))
        o = jnp.dot(p.astype(BF16), vc[:, lanes], preferred_element_type=F32)
        outs.append(_unstack_heads(o * (1.0 / jnp.sum(p, axis=-1, keepdims=True)), n, masks))
    ya_s[...] = (jnp.concatenate(outs, axis=-1) * sa).astype(BF16)

    sb_s[...] = (_proj_col(h, w_ref, 4) * _silu(_proj_col(h, w_ref, 7))).astype(BF16)
    sc_s[...] = _silu(_proj_col(h, w_ref, 10)).astype(BF16)
    zeros = jnp.zeros((CONV_HALO, D_BRANCH), F32)
    for win, val in ((swin, _proj_col(h, w_ref, 5) * _proj_col(h, w_ref, 6)),
                     (cwin, _proj_col(h, w_ref, 8) * jax.nn.sigmoid(_proj_col(h, w_ref, 9)))):
        win[0:CONV_HALO, :] = zeros
        win[CONV_HALO:CONV_HALO + n, :] = val
        win[CONV_HALO + n:, :] = zeros
    _conv_branches(cwin, swin, sb_s, sc_s, wsc_ref, wcc_ref, bcc_ref, lng_ref, lnb_ref, yb_s, yc_s, n)
    for i in range(2 * N_BRANCH):
        gates_s[:, i * D_BRANCH:(i + 1) * D_BRANCH] = _proj_col(h, w_ref, 11 + i).astype(BF16)
    xc_o[...] = _merge(xc, gate_ref[...], ya_s[...], yb_s[...], yc_s[...],
                       gates_s, woa_ref, wob_ref, woc_ref, wo_ref)


def _ctx_call(l, update, xc, mods, ctx_row, p):
    bsz, n, d = xc.shape
    tok = lambda w: pl.BlockSpec((None, n, w), lambda b: (b, 0, 0))
    names = ("qg", "kg", "wsc", "wcc", "bcc", "lng", "lnb", "woa", "wob", "woc", "wo")
    out_specs = [tok(D_BRANCH), tok(D_BRANCH)]
    out_shape = [jax.ShapeDtypeStruct((bsz, n, D_BRANCH), BF16)] * 2
    scratch = []
    if update:
        out_specs.append(tok(d))
        out_shape.append(jax.ShapeDtypeStruct((bsz, n, d), F32))
        scratch = [pltpu.VMEM((n + 2 * CONV_HALO, D_BRANCH), F32)] * 2 \
            + [pltpu.VMEM((n, D_BRANCH), BF16)] * 5 + [pltpu.VMEM((n, N_BRANCH * d), BF16)]
    return pl.pallas_call(
        functools.partial(_ctx_kernel, update),
        grid=(bsz,),
        in_specs=[tok(d), _layer_spec(p["norm_g"], l), _mod_spec(mods, l, 1, ctx_row), _mod_spec(mods, l, 0, ctx_row),
                  _mod_spec(mods, l, 2, ctx_row), _layer_spec(p["w_in"], l)]
        + [_layer_spec(p[n], l) for n in names],
        out_specs=out_specs,
        out_shape=out_shape,
        scratch_shapes=scratch,
        compiler_params=pltpu.CompilerParams(
            dimension_semantics=("parallel",), vmem_limit_bytes=VMEM_LIMIT),
        name="ctx_layer_update" if update else "ctx_layer_kv",
    )(xc, p["norm_g"], mods, mods, mods, p["w_in"], *(p[n] for n in names))


def _bias_tables(rpb):
    depth, n_heads, n_drow, n_dcol = rpb.shape
    qc = np.arange(GRID_W)[:, None]
    kc = np.arange(GRID_W)[None, :]
    col_start = np.clip(qc - WIN_COLS // 2, 0, GRID_W - WIN_COLS)
    valid = (kc - col_start >= 0) & (kc - col_start < WIN_COLS)
    period = 2 * GRID_W
    lead = GRID_W - WIN_COLS
    e = jnp.pad(rpb, ((0, 0), (0, 0), (0, 0), (lead, period - lead - n_dcol)))
    t = jnp.tile(e, (1, 1, 1, GRID_W))[..., :GRID_W * (period - 1)]
    t = t.reshape(depth, n_heads, n_drow, GRID_W, period - 1)[..., GRID_W - 1:]
    a = jnp.where(valid, t, NEG_INF)
    pair = jnp.concatenate([a[:, :, :N_BIAS_IDX], a[:, :, 1:N_BIAS_IDX + 1]], axis=-1)
    pair = pair.reshape(depth, N_HGROUPS, HEADS_PER_GROUP, N_BIAS_IDX, GRID_W, 2 * GRID_W)
    return jnp.transpose(pair, (0, 1, 3, 2, 4, 5)).reshape(
        depth, N_HGROUPS, N_BIAS_IDX, HEADS_PER_GROUP * GRID_W, 2 * GRID_W)


def kernel(x, c, ctx, c_ctx, norm_g, w_ada, b_ada, w_in, q_norm_g, k_norm_g, rpb, conv_short_w,
           conv_conf_w, conv_conf_b, ln_conf_g, ln_conf_b, w_out_a, w_out_b, w_out_c, w_o):
    depth = w_in.shape[0]
    bsz, seq, d = x.shape
    assert (d, seq % TM_MIX, seq % TM_PROJ) == (D_MODEL, 0, 0)

    n_cond = 8
    cond = jnp.concatenate([c, c_ctx[None], jnp.zeros((n_cond - bsz - 1, d), F32)], axis=0)
    mods = _ada_call(cond, w_ada, b_ada).reshape(depth, n_cond, 3, 1, d)
    row = lambda t: t.reshape(depth, 1, t.shape[-1])
    p = {
        "norm_g": row(norm_g), "w_in": w_in.astype(BF16),
        "qg": row(jnp.tile(q_norm_g, (1, N_HEADS))), "kg": row(jnp.tile(k_norm_g, (1, N_HEADS))),
        "wsc": conv_short_w, "wcc": conv_conf_w, "bcc": row(conv_conf_b),
        "lng": row(ln_conf_g), "lnb": row(ln_conf_b),
        "woa": w_out_a.astype(BF16), "wob": w_out_b.astype(BF16), "woc": w_out_c.astype(BF16),
        "wo": w_o.astype(BF16),
    }
    bias = _bias_tables(rpb)

    xc = ctx
    for l in range(depth):
        update = l < depth - 1
        ctx_out = _ctx_call(l, update, xc, mods, bsz, p)
        kc, vc = ctx_out[0], ctx_out[1]
        q, k, v, sa, yb, yc, gates = _inproj_call(l, x, mods, p)
        x = _mixer_call(l, x, q, k, v, kc, vc, bias, sa, yb, yc, gates, mods, p)
        if update:
            xc = ctx_out[2]
    return x
```

```python
import functools

import numpy as np
import jax
import jax.numpy as jnp
from jax import lax
from jax.experimental import pallas as pl
from jax.experimental.pallas import tpu as pltpu

F32 = jnp.float32
BF16 = jnp.bfloat16

D_MODEL = 1024
D_BRANCH = 512
HEAD_DIM = 64
N_HEADS = 8
GRID_W = 64
WIN_ROWS = 8
WIN_COLS = 16
SHORT_W = 3
CONF_W = 31
N_BRANCH = 3
N_PROJ_GROUPS = 11 + 2 * N_BRANCH
EPS = 1e-6
NEG_INF = -1e30

HEADS_PER_GROUP = 4
GROUP_W = HEADS_PER_GROUP * HEAD_DIM
N_HGROUPS = N_HEADS // HEADS_PER_GROUP
N_BIAS_IDX = 2 * WIN_ROWS - 2
CONV_HALO = 16

TM_PROJ = 512
ROWS_MIX = 8
TM_MIX = ROWS_MIX * GRID_W
KV_HALO = 4 * GRID_W
CONV_CHUNK = 64
LANES = 128
SUBLANES = 8

VMEM_LIMIT = 56 * 1024 * 1024


def _silu(x):
    return x * jax.nn.sigmoid(x)


def _modulate(x, g, scale, shift):
    y = x * lax.rsqrt(jnp.mean(x * x, axis=-1, keepdims=True) + EPS)
    return ((y * g) * (1.0 + scale) + shift).astype(BF16)


def _head_rms(z, g):
    low = lax.broadcasted_iota(jnp.int32, (1, LANES), 1) < HEAD_DIM
    outs = []
    for c0 in range(0, D_BRANCH, LANES):
        blk = z[:, c0:c0 + LANES]
        sq = blk * blk
        s_lo = jnp.sum(jnp.where(low, sq, 0.0), axis=-1, keepdims=True)
        s_hi = jnp.sum(jnp.where(low, 0.0, sq), axis=-1, keepdims=True)
        ms = jnp.where(low, s_lo, s_hi) * (1.0 / HEAD_DIM)
        outs.append(blk * lax.rsqrt(ms + EPS) * g[:, c0:c0 + LANES])
    return jnp.concatenate(outs, axis=-1)


def _proj_col(h, w_ref, i):
    return jnp.dot(h, w_ref[:, i * D_BRANCH:(i + 1) * D_BRANCH], preferred_element_type=F32)


def _head_masks():
    lane = lax.broadcasted_iota(jnp.int32, (1, GROUP_W), 1) // HEAD_DIM
    return [lane == h for h in range(HEADS_PER_GROUP)]


def _stack_heads(qg, masks):
    return jnp.concatenate([jnp.where(m, qg, jnp.zeros_like(qg)) for m in masks], axis=0)


def _unstack_heads(o, n, masks):
    out = jnp.where(masks[0], o[0:n], 0.0)
    for h in range(1, HEADS_PER_GROUP):
        out = out + jnp.where(masks[h], o[h * n:(h + 1) * n], 0.0)
    return out


def _dot_t(a, b):
    return lax.dot_general(a, b, (((1,), (1,)), ((), ())), preferred_element_type=F32)


def _ordering_zero(v):
    u = pltpu.bitcast(v, jnp.uint32)
    return pltpu.bitcast(lax.shift_right_logical(lax.shift_right_logical(u, jnp.uint32(16)), jnp.uint32(16)), F32)


def _dwconv(win_ref, w_ref, width, r0, n, c0, after=None):
    half = (width - 1) // 2
    lanes = slice(c0, c0 + LANES)
    offsets = [CONV_HALO + j - half for j in range(width)]
    span = n + SUBLANES * (max(offsets) // SUBLANES + 1)
    base = win_ref[r0:r0 + span, lanes]
    if after is not None:
        base = base + jnp.tile(after, (span // SUBLANES, 1))
    acc = None
    for s in range(SUBLANES):
        taps = [(j, o // SUBLANES) for j, o in enumerate(offsets) if o % SUBLANES == s]
        if not taps:
            continue
        shifted = pltpu.roll(base, span - s, 0) if s else base
        for j, q in taps:
            term = shifted[SUBLANES * q:SUBLANES * q + n] * w_ref[j:j + 1, lanes]
            acc = term if acc is None else acc + term
    return acc


def _layer_norm(u, g, b):
    mu = jnp.mean(u, axis=-1, keepdims=True)
    d = u - mu
    var = jnp.mean(d * d, axis=-1, keepdims=True)
    return d * lax.rsqrt(var + EPS) * g + b


def _short_conv_chunk(swin, sb_ref, wsc_ref, yb_ref, r0, n):
    rows = slice(r0, r0 + n)
    for c0 in range(0, D_BRANCH, LANES):
        yb = _dwconv(swin, wsc_ref, SHORT_W, r0, n, c0) * sb_ref[rows, c0:c0 + LANES].astype(F32)
        yb_ref[rows, c0:c0 + LANES] = yb.astype(BF16)


def _conf_conv_lanes(cwin, wcc_ref, r0, n, lane_blocks, after=None):
    return [_dwconv(cwin, wcc_ref, CONF_W, r0, n, b * LANES, after) for b in lane_blocks]


def _conf_finish(u_blocks, sc_ref, bcc_ref, lng_ref, lnb_ref, yc_ref, r0, n):
    rows = slice(r0, r0 + n)
    u = jnp.concatenate(u_blocks, axis=-1)
    u = _silu(_layer_norm(u + bcc_ref[...], lng_ref[...], lnb_ref[...]))
    yc = u * sc_ref[rows, :].astype(F32)
    yc_ref[rows, :] = yc.astype(BF16)
    return yc


def _conf_conv_chunk(cwin, sc_ref, wcc_ref, bcc_ref, lng_ref, lnb_ref, yc_ref, r0, n):
    u_blocks = _conf_conv_lanes(cwin, wcc_ref, r0, n, range(D_BRANCH // LANES))
    return _conf_finish(u_blocks, sc_ref, bcc_ref, lng_ref, lnb_ref, yc_ref, r0, n)


def _conv_branches(cwin, swin, sb_ref, sc_ref, wsc_ref, wcc_ref, bcc_ref, lng_ref, lnb_ref,
                   yb_ref, yc_ref, n_tokens):
    for r0 in range(0, n_tokens, CONV_CHUNK):
        n = min(CONV_CHUNK, n_tokens - r0)
        _short_conv_chunk(swin, sb_ref, wsc_ref, yb_ref, r0, n)
        _conf_conv_chunk(cwin, sc_ref, wcc_ref, bcc_ref, lng_ref, lnb_ref, yc_ref, r0, n)


def _merge(x, gate, ya, yb, yc, gates_ref, woa_ref, wob_ref, woc_ref, wo_ref):
    d = D_MODEL
    branch_gate = lambda j: jax.nn.sigmoid(gates_ref[:, j * d:(j + 1) * d].astype(F32))
    m = branch_gate(0) * jnp.dot(ya, woa_ref[...], preferred_element_type=F32)
    m = m + branch_gate(1) * jnp.dot(yb, wob_ref[...], preferred_element_type=F32)
    m = m + branch_gate(2) * jnp.dot(yc, woc_ref[...], preferred_element_type=F32)
    return x + gate * jnp.dot(m.astype(BF16), wo_ref[...], preferred_element_type=F32)


def _ada_kernel(cond_ref, w_ref, b_ref, o_ref):
    o_ref[...] = jnp.dot(_silu(cond_ref[...]), w_ref[...], preferred_element_type=F32) + b_ref[...]


def _ada_call(cond, w_ada, b_ada):
    depth = w_ada.shape[0]
    n_rows = cond.shape[0]
    return pl.pallas_call(
        _ada_kernel,
        grid=(depth, 3),
        in_specs=[
            pl.BlockSpec((n_rows, D_MODEL), lambda l, j: (0, 0)),
            pl.BlockSpec((None, D_MODEL, D_MODEL), lambda l, j: (l, 0, j)),
            pl.BlockSpec((None, 1, D_MODEL), lambda l, j: (l, 0, j)),
        ],
        out_specs=pl.BlockSpec((None, n_rows, D_MODEL), lambda l, j: (l, 0, j)),
        out_shape=jax.ShapeDtypeStruct((depth, n_rows, 3 * D_MODEL), F32),
        name="ada_mod",
    )(cond, w_ada, b_ada.reshape(depth, 1, 3 * D_MODEL))


def _inproj_kernel(xp_ref, x_ref, xn_ref, g_ref, scale_ref, shift_ref, w_ref, qg_ref, kg_ref,
                   wsc_ref, wcc_ref, bcc_ref, lng_ref, lnb_ref,
                   q_o, k_o, v_o, sa_o, yb_o, yc_o, gates_o, h_s, cwin, swin, sb_s, sc_s):
    i = pl.program_id(1)
    mod = (g_ref[...], scale_ref[...], shift_ref[...])
    tile = slice(CONV_HALO, CONV_HALO + TM_PROJ)
    h_s[0:CONV_HALO, :] = _modulate(xp_ref[...], *mod)
    h_s[tile, :] = _modulate(x_ref[...], *mod)
    h_s[CONV_HALO + TM_PROJ:, :] = _modulate(xn_ref[...], *mod)
    col = lambda c: _proj_col(h_s[tile, :], w_ref, c)
    col_ext = lambda c: _proj_col(h_s[...], w_ref, c)

    has_prev = (i > 0).astype(F32)
    has_next = (i < pl.num_programs(1) - 1).astype(F32)

    def fill_window(win, val):
        win[0:CONV_HALO, :] = val[0:CONV_HALO] * has_prev
        win[tile, :] = val[tile]
        win[CONV_HALO + TM_PROJ:, :] = val[CONV_HALO + TM_PROJ:] * has_next

    started = lambda z: _ordering_zero(z[0:SUBLANES, 0:LANES])
    probe = (slice(CONV_HALO, 2 * CONV_HALO), slice(0, LANES))
    n_lane_blocks = D_BRANCH // LANES
    conv_parts, done = {}, {}

    def conf_half(hf, after):
        c, second = divmod(hf, 2)
        r0 = c * CONV_CHUNK
        blocks = range(n_lane_blocks // 2, n_lane_blocks) if second else range(n_lane_blocks // 2)
        u = _conf_conv_lanes(cwin, wcc_ref, r0, CONV_CHUNK, blocks, after)
        if second:
            yc = _conf_finish(conv_parts.pop(c) + u, sc_s, bcc_ref, lng_ref, lnb_ref, yc_o, r0, CONV_CHUNK)
            done[hf] = yc[0:CONV_HALO, 0:LANES]
        else:
            conv_parts[c] = u
            done[hf] = u[0][0:CONV_HALO]

    def wait_half(hf):
        h_s[probe] = h_s[probe] + _ordering_zero(done.pop(hf)).astype(BF16)

    def short_chunks(*cs):
        for c in cs:
            _short_conv_chunk(swin, sb_s, wsc_ref, yb_o, c * CONV_CHUNK, CONV_CHUNK)

    pending = {}

    def group_cv_a():
        pending["z5"] = col_ext(5)
        return pending["z5"]

    def group_cv_b():
        z6 = col_ext(6)
        fill_window(swin, pending.pop("z5") * z6)
        return z6

    def group_sb_a():
        pending["z4"] = col(4)
        return pending["z4"]

    def group_sb_b():
        z7 = col(7)
        sb_s[...] = pending.pop("z4") * _silu(z7)
        return z7

    def group_q():
        z = col(0)
        q_o[...] = (_head_rms(z, qg_ref[...]) * HEAD_DIM ** -0.5).astype(BF16)
        return z

    def group_k():
        z = col(1)
        k_o[...] = _head_rms(z, kg_ref[...]).astype(BF16)
        return z

    def group_v():
        z = col(2)
        v_o[...] = z.astype(BF16)
        return z

    def group_sa():
        z = col(3)
        sa_o[...] = z.astype(BF16)
        return z

    def gate_group(j):
        z = col(11 + j)
        gates_o[:, j * D_BRANCH:(j + 1) * D_BRANCH] = z.astype(BF16)
        return z

    groups = [group_cv_a, group_cv_b, group_sb_a, group_sb_b, group_q, group_k, group_v, group_sa]
    groups += [functools.partial(gate_group, j) for j in range(2 * N_BRANCH)]
    waits = {k: [k - 1] for k in range(1, 11)}
    waits.update({11: [10, 11], 12: [12, 13], 13: [14]})
    anchored = {k: [k + 1] for k in range(0, 9)}
    anchored.update({9: [10, 11], 10: [12, 13], 11: [14], 12: [15]})
    shorts = {5: (0, 1), 7: (2, 3), 9: (4, 5), 11: (6, 7)}

    fill_window(cwin, col_ext(8) * jax.nn.sigmoid(col_ext(9)))
    z10 = col(10)
    sc_s[...] = _silu(z10)
    conf_half(0, started(z10))
    for k, group in enumerate(groups):
        for hf in waits.get(k, ()):
            wait_half(hf)
        z = group()
        for hf in anchored.get(k, ()):
            conf_half(hf, started(z))
        short_chunks(*shorts.get(k, ()))


def _layer_spec(arr, l):
    nd = arr.ndim
    return pl.BlockSpec((None,) + arr.shape[1:], lambda *_: (l,) + (0,) * (nd - 1),
                        pipeline_mode=pl.Buffered(1))


def _mod_spec(mods, l, part, row=None):
    d = mods.shape[-1]
    if row is None:
        return pl.BlockSpec((None, None, None, 1, d), lambda b, *_: (l, b, part, 0, 0))
    return pl.BlockSpec((None, None, None, 1, d), lambda *_: (l, row, part, 0, 0))


def _inproj_call(l, x, mods, p):
    bsz, seq, d = x.shape
    halo_per_tile = TM_PROJ // CONV_HALO
    n_halo_blocks = seq // CONV_HALO
    tok = lambda w: pl.BlockSpec((None, TM_PROJ, w), lambda b, i: (b, i, 0))
    x_prev = pl.BlockSpec((None, CONV_HALO, d), lambda b, i: (b, jnp.maximum(i * halo_per_tile - 1, 0), 0))
    x_next = pl.BlockSpec((None, CONV_HALO, d),
                          lambda b, i: (b, jnp.minimum((i + 1) * halo_per_tile, n_halo_blocks - 1), 0))
    names = ("qg", "kg", "wsc", "wcc", "bcc", "lng", "lnb")
    widths = [D_BRANCH] * 6 + [N_BRANCH * D_MODEL]
    return pl.pallas_call(
        _inproj_kernel,
        grid=(bsz, seq // TM_PROJ),
        in_specs=[x_prev, tok(d), x_next, _layer_spec(p["norm_g"], l), _mod_spec(mods, l, 1), _mod_spec(mods, l, 0),
                  _layer_spec(p["w_in"], l)] + [_layer_spec(p[n], l) for n in names],
        out_specs=[tok(w) for w in widths],
        out_shape=[jax.ShapeDtypeStruct((bsz, seq, w), BF16) for w in widths],
        scratch_shapes=[pltpu.VMEM((TM_PROJ + 2 * CONV_HALO, d), BF16)]
        + [pltpu.VMEM((TM_PROJ + 2 * CONV_HALO, D_BRANCH), F32)] * 2
        + [pltpu.VMEM((TM_PROJ, D_BRANCH), F32)] * 2,
        compiler_params=pltpu.CompilerParams(
            dimension_semantics=("parallel", "parallel"), vmem_limit_bytes=VMEM_LIMIT),
        name="in_proj",
    )(x, x, x, p["norm_g"], mods, mods, p["w_in"], *(p[n] for n in names))


def _mixer_kernel(x_ref, q_ref, kwin, vwin, kc_ref, vc_ref,
                  bias_ref, sa_ref, yb_ref, yc_ref, gates_ref, gate_ref,
                  woa_ref, wob_ref, woc_ref, wo_ref, out_ref, ya_ref):
    i = pl.program_id(1)
    n_rows = pl.num_programs(1) * ROWS_MIX

    masks = _head_masks()
    tile_row0 = i * ROWS_MIX
    win_row0 = jnp.clip(tile_row0 - WIN_ROWS // 2, 0, n_rows - (ROWS_MIX + WIN_ROWS))

    def attend_row(rr, carry):
        row = tile_row0 + rr
        row_start = jnp.clip(row - WIN_ROWS // 2, 0, n_rows - WIN_ROWS)
        delta = row - row_start
        k0 = pl.multiple_of((row_start - win_row0) * GRID_W, GRID_W)
        q0 = pl.multiple_of(rr * GRID_W, GRID_W)
        q_row = q_ref[pl.ds(q0, GRID_W), :]
        outs = []
        for g in range(N_HGROUPS):
            lanes = slice(g * GROUP_W, (g + 1) * GROUP_W)
            qm = _stack_heads(q_row[:, lanes], masks)
            s_loc = _dot_t(qm, kwin[pl.ds(k0, WIN_ROWS * GRID_W), lanes])
            s_ctx = _dot_t(qm, kc_ref[:, lanes])
            bias = jnp.concatenate(
                [bias_ref[g, 2 * j + WIN_ROWS - 1 - delta] for j in range(WIN_ROWS // 2)], axis=-1)
            s_loc = s_loc + bias
            m = jnp.maximum(jnp.max(s_loc, axis=-1, keepdims=True), jnp.max(s_ctx, axis=-1, keepdims=True))
            p_loc = jnp.exp(s_loc - m)
            p_ctx = jnp.exp(s_ctx - m)
            denom = jnp.sum(p_loc, axis=-1, keepdims=True) + jnp.sum(p_ctx, axis=-1, keepdims=True)
            o = jnp.dot(p_loc.astype(BF16), vwin[pl.ds(k0, WIN_ROWS * GRID_W), lanes],
                        preferred_element_type=F32)
            o = o + jnp.dot(p_ctx.astype(BF16), vc_ref[:, lanes], preferred_element_type=F32)
            outs.append(_unstack_heads(o * (1.0 / denom), GRID_W, masks))
        att = jnp.concatenate(outs, axis=-1)
        ya_ref[pl.ds(q0, GRID_W), :] = (att * _silu(sa_ref[pl.ds(q0, GRID_W), :].astype(F32))).astype(BF16)
        return carry

    lax.fori_loop(0, ROWS_MIX, attend_row, 0, unroll=True)

    out_ref[...] = _merge(x_ref[...], gate_ref[...], ya_ref[...], yb_ref[...], yc_ref[...],
                          gates_ref, woa_ref, wob_ref, woc_ref, wo_ref)


def _mixer_call(l, x, q, k, v, kc, vc, bias, sa, yb, yc, gates, mods, p):
    bsz, seq, d = x.shape
    n_kv_blocks = seq // KV_HALO
    kv_per_tile = TM_MIX // KV_HALO
    ctx_len = kc.shape[1]

    tok = lambda w: pl.BlockSpec((None, TM_MIX, w), lambda b, i: (b, i, 0))
    win = TM_MIX + 2 * KV_HALO
    kv_win = pl.BlockSpec(
        (pl.Element(win), pl.Element(D_BRANCH)),
        lambda b, i: (pl.multiple_of(b * seq + jnp.clip(i * TM_MIX - KV_HALO, 0, seq - win), KV_HALO), 0))
    ctx_spec = pl.BlockSpec((None, ctx_len, D_BRANCH), lambda b, i: (b, 0, 0))
    tb = tok(D_BRANCH)
    names = ("woa", "wob", "woc", "wo")

    in_specs = [tok(d), tb, kv_win, kv_win, ctx_spec, ctx_spec,
                _layer_spec(bias, l), tb, tb, tb, tok(N_BRANCH * d), _mod_spec(mods, l, 2)] \
        + [_layer_spec(p[n], l) for n in names]
    return pl.pallas_call(
        _mixer_kernel,
        grid=(bsz, seq // TM_MIX),
        in_specs=in_specs,
        out_specs=tok(d),
        out_shape=jax.ShapeDtypeStruct((bsz, seq, d), F32),
        scratch_shapes=[pltpu.VMEM((TM_MIX, D_BRANCH), BF16)],
        compiler_params=pltpu.CompilerParams(
            dimension_semantics=("parallel", "parallel"), vmem_limit_bytes=VMEM_LIMIT),
        name="mixer",
    )(x, q, k.reshape(bsz * seq, D_BRANCH), v.reshape(bsz * seq, D_BRANCH), kc, vc, bias, sa, yb, yc, gates, mods, *(p[n] for n in names))


def _ctx_kernel(update, xc_ref, g_ref, scale_ref, shift_ref, gate_ref, w_ref, qg_ref, kg_ref,
                wsc_ref, wcc_ref, bcc_ref, lng_ref, lnb_ref, woa_ref, wob_ref, woc_ref, wo_ref,
                *refs):
    if update:
        kc_o, vc_o, xc_o, cwin, swin, sb_s, sc_s, ya_s, yb_s, yc_s, gates_s = refs
    else:
        kc_o, vc_o = refs
    n = xc_ref.shape[0]
    xc = xc_ref[...]
    h = _modulate(xc, g_ref[...], scale_ref[...], shift_ref[...])
    kc = _head_rms(_proj_col(h, w_ref, 1), kg_ref[...]).astype(BF16)
    vc = _proj_col(h, w_ref, 2).astype(BF16)
    kc_o[...] = kc
    vc_o[...] = vc
    if not update:
        return

    q = (_head_rms(_proj_col(h, w_ref, 0), qg_ref[...]) * HEAD_DIM ** -0.5).astype(BF16)
    sa = _silu(_proj_col(h, w_ref, 3))
    masks = _head_masks()
    outs = []
    for g in range(N_HGROUPS):
        lanes = slice(g * GROUP_W, (g + 1) * GROUP_W)
        qm = _stack_heads(q[:, lanes], masks)
        s = _dot_t(qm, kc[:, lanes])
        p = jnp.exp(s - jnp.max(s, axis=-1, keepdims=True))
        o = jnp.dot(p.astype(BF16), vc[:, lanes], preferred_element_type=F32)
        outs.append(_unstack_heads(o * (1.0 / jnp.sum(p, axis=-1, keepdims=True)), n, masks))
    ya_s[...] = (jnp.concatenate(outs, axis=-1) * sa).astype(BF16)

    sb_s[...] = (_proj_col(h, w_ref, 4) * _silu(_proj_col(h, w_ref, 7))).astype(BF16)
    sc_s[...] = _silu(_proj_col(h, w_ref, 10)).astype(BF16)
    zeros = jnp.zeros((CONV_HALO, D_BRANCH), F32)
    for win, val in ((swin, _proj_col(h, w_ref, 5) * _proj_col(h, w_ref, 6)),
                     (cwin, _proj_col(h, w_ref, 8) * jax.nn.sigmoid(_proj_col(h, w_ref, 9)))):
        win[0:CONV_HALO, :] = zeros
        win[CONV_HALO:CONV_HALO + n, :] = val
        win[CONV_HALO + n:, :] = zeros
    _conv_branches(cwin, swin, sb_s, sc_s, wsc_ref, wcc_ref, bcc_ref, lng_ref, lnb_ref, yb_s, yc_s, n)
    for i in range(2 * N_BRANCH):
        gates_s[:, i * D_BRANCH:(i + 1) * D_BRANCH] = _proj_col(h, w_ref, 11 + i).astype(BF16)
    xc_o[...] = _merge(xc, gate_ref[...], ya_s[...], yb_s[...], yc_s[...],
                       gates_s, woa_ref, wob_ref, woc_ref, wo_ref)


def _ctx_call(l, update, xc, mods, ctx_row, p):
    bsz, n, d = xc.shape
    tok = lambda w: pl.BlockSpec((None, n, w), lambda b: (b, 0, 0))
    names = ("qg", "kg", "wsc", "wcc", "bcc", "lng", "lnb", "woa", "wob", "woc", "wo")
    out_specs = [tok(D_BRANCH), tok(D_BRANCH)]
    out_shape = [jax.ShapeDtypeStruct((bsz, n, D_BRANCH), BF16)] * 2
    scratch = []
    if update:
        out_specs.append(tok(d))
        out_shape.append(jax.ShapeDtypeStruct((bsz, n, d), F32))
        scratch = [pltpu.VMEM((n + 2 * CONV_HALO, D_BRANCH), F32)] * 2 \
            + [pltpu.VMEM((n, D_BRANCH), BF16)] * 5 + [pltpu.VMEM((n, N_BRANCH * d), BF16)]
    return pl.pallas_call(
        functools.partial(_ctx_kernel, update),
        grid=(bsz,),
        in_specs=[tok(d), _layer_spec(p["norm_g"], l), _mod_spec(mods, l, 1, ctx_row), _mod_spec(mods, l, 0, ctx_row),
                  _mod_spec(mods, l, 2, ctx_row), _layer_spec(p["w_in"], l)]
        + [_layer_spec(p[n], l) for n in names],
        out_specs=out_specs,
        out_shape=out_shape,
        scratch_shapes=scratch,
        compiler_params=pltpu.CompilerParams(
            dimension_semantics=("parallel",), vmem_limit_bytes=VMEM_LIMIT),
        name="ctx_layer_update" if update else "ctx_layer_kv",
    )(xc, p["norm_g"], mods, mods, mods, p["w_in"], *(p[n] for n in names))


def _bias_tables(rpb):
    depth, n_heads, n_drow, n_dcol = rpb.shape
    qc = np.arange(GRID_W)[:, None]
    kc = np.arange(GRID_W)[None, :]
    col_start = np.clip(qc - WIN_COLS // 2, 0, GRID_W - WIN_COLS)
    valid = (kc - col_start >= 0) & (kc - col_start < WIN_COLS)
    period = 2 * GRID_W
    lead = GRID_W - WIN_COLS
    e = jnp.pad(rpb, ((0, 0), (0, 0), (0, 0), (lead, period - lead - n_dcol)))
    t = jnp.tile(e, (1, 1, 1, GRID_W))[..., :GRID_W * (period - 1)]
    t = t.reshape(depth, n_heads, n_drow, GRID_W, period - 1)[..., GRID_W - 1:]
    a = jnp.where(valid, t, NEG_INF)
    pair = jnp.concatenate([a[:, :, :N_BIAS_IDX], a[:, :, 1:N_BIAS_IDX + 1]], axis=-1)
    pair = pair.reshape(depth, N_HGROUPS, HEADS_PER_GROUP, N_BIAS_IDX, GRID_W, 2 * GRID_W)
    return jnp.transpose(pair, (0, 1, 3, 2, 4, 5)).reshape(
        depth, N_HGROUPS, N_BIAS_IDX, HEADS_PER_GROUP * GRID_W, 2 * GRID_W)


def kernel(x, c, ctx, c_ctx, norm_g, w_ada, b_ada, w_in, q_norm_g, k_norm_g, rpb, conv_short_w,
           conv_conf_w, conv_conf_b, ln_conf_g, ln_conf_b, w_out_a, w_out_b, w_out_c, w_o):
    depth = w_in.shape[0]
    bsz, seq, d = x.shape
    assert (d, seq % TM_MIX, seq % TM_PROJ) == (D_MODEL, 0, 0)

    n_cond = 8
    cond = jnp.concatenate([c, c_ctx[None], jnp.zeros((n_cond - bsz - 1, d), F32)], axis=0)
    mods = _ada_call(cond, w_ada, b_ada).reshape(depth, n_cond, 3, 1, d)
    row = lambda t: t.reshape(depth, 1, t.shape[-1])
    p = {
        "norm_g": row(norm_g), "w_in": w_in.astype(BF16),
        "qg": row(jnp.tile(q_norm_g, (1, N_HEADS))), "kg": row(jnp.tile(k_norm_g, (1, N_HEADS))),
        "wsc": conv_short_w, "wcc": conv_conf_w, "bcc": row(conv_conf_b),
        "lng": row(ln_conf_g), "lnb": row(ln_conf_b),
        "woa": w_out_a.astype(BF16), "wob": w_out_b.astype(BF16), "woc": w_out_c.astype(BF16),
        "wo": w_o.astype(BF16),
    }
    bias = _bias_tables(rpb)

    xc = ctx
    for l in range(depth):
        update = l < depth - 1
        ctx_out = _ctx_call(l, update, xc, mods, bsz, p)
        kc, vc = ctx_out[0], ctx_out[1]
        q, k, v, sa, yb, yc, gates = _inproj_call(l, x, mods, p)
        x = _mixer_call(l, x, q, k, v, kc, vc, bias, sa, yb, yc, gates, mods, p)
        if update:
            xc = ctx_out[2]
    return x
```
